```python
import math
import jax, jax.numpy as jnp
from jax import lax
import numpy as np

D_MODEL = 1024
BATCH = 8
SEQ = 2048
DEPTH = 2
DEC_BATCH = 32
DEC_SEQ = 8
PAST_LEN = 16384
PAGE_SIZE = 128

DIFF_HEADS = 4
DIFF_QK_DIM = 64
DIFF_V_DIM = 2 * DIFF_QK_DIM
DSA_HEADS = 8
DSA_HEAD_DIM = 64
IDX_HEADS = 4
IDX_DIM = 64
DSA_TOPK = 256
MLSTM_HEADS = 4
MLSTM_QK_DIM = D_MODEL // (2 * MLSTM_HEADS)
MLSTM_V_DIM = D_MODEL // MLSTM_HEADS
MLSTM_CHUNK = 64
IGATE_SOFTCAP = 15.0
D_FF = 4 * D_MODEL
ROPE_THETA = 10000.0
RMS_EPS = 1e-6
Q_BLOCK = 128
DSA_Q_BLOCK = 64
N_ATTN_LAYERS = (DEPTH + 1) // 2
N_MLSTM_LAYERS = DEPTH // 2
POOL_NUM = 5
POOL_DEN = 4
EVEN_SPLITS = (DIFF_HEADS * 2 * DIFF_QK_DIM, DIFF_HEADS * 2 * DIFF_QK_DIM, DIFF_HEADS * DIFF_V_DIM,
               DSA_HEADS * DSA_HEAD_DIM, DSA_HEADS * DSA_HEAD_DIM, DSA_HEADS * DSA_HEAD_DIM,
               IDX_HEADS * IDX_DIM, IDX_DIM, IDX_HEADS)
EVEN_IN = sum(EVEN_SPLITS)
EVEN_OUT = DIFF_HEADS * DIFF_V_DIM + DSA_HEADS * DSA_HEAD_DIM
ODD_SPLITS = (MLSTM_HEADS * MLSTM_QK_DIM, MLSTM_HEADS * MLSTM_QK_DIM, MLSTM_HEADS * MLSTM_V_DIM,
              MLSTM_HEADS * MLSTM_V_DIM, MLSTM_HEADS, MLSTM_HEADS)
ODD_IN = sum(ODD_SPLITS)
ODD_OUT = MLSTM_HEADS * MLSTM_V_DIM
F32 = jnp.float32

kernel_name = 'hybrid_diffattn_dsa_mlstm_step'


def _split(x, sizes):
    out, start = [], 0
    for s in sizes:
        out.append(x[..., start:start + s])
        start += s
    return out


def rms_norm(x, g):
    xf = x.astype(F32)
    y = xf * lax.rsqrt(jnp.mean(xf * xf, axis=-1, keepdims=True) + RMS_EPS)
    return (y * g.astype(F32)).astype(x.dtype)


def rope(x, pos):
    half = x.shape[-1] // 2
    inv = ROPE_THETA ** (-jnp.arange(half, dtype=F32) / half)
    ang = pos.astype(F32)[:, None] * inv[None, :]
    shape = (pos.shape[0],) + (1,) * (x.ndim - 3) + (half,)
    cos = jnp.cos(ang).reshape(shape)
    sin = jnp.sin(ang).reshape(shape)
    x1 = x[..., :half].astype(F32)
    x2 = x[..., half:].astype(F32)
    return jnp.concatenate([x1 * cos - x2 * sin, x2 * cos + x1 * sin], axis=-1).astype(x.dtype)


def sqrelu_ffn(h, w_up, w_down):
    u = jax.nn.relu(jnp.einsum('btd,df->btf', h, w_up))
    return jnp.einsum('btf,fd->btd', u * u, w_down)


def gather_rows(rows, idx):
    return jax.vmap(lambda r, i: r[i])(rows, idx)


def attn_project(h, pos, w_in):
    B, T, _ = h.shape
    proj = jnp.einsum('btd,de->bte', h, w_in)
    aq, ak, av, bq, bk, bv, iq, ik, iw = _split(proj, EVEN_SPLITS)
    aq = rope(aq.reshape(B, T, DIFF_HEADS, 2, DIFF_QK_DIM), pos)
    ak = rope(ak.reshape(B, T, DIFF_HEADS, 2, DIFF_QK_DIM), pos)
    av = av.reshape(B, T, DIFF_HEADS, DIFF_V_DIM)
    bq = rope(bq.reshape(B, T, DSA_HEADS, DSA_HEAD_DIM), pos)
    bk = rope(bk.reshape(B, T, DSA_HEADS, DSA_HEAD_DIM), pos)
    bv = bv.reshape(B, T, DSA_HEADS, DSA_HEAD_DIM)
    iq = rope(iq.reshape(B, T, IDX_HEADS, IDX_DIM), pos)
    ik = rope(ik, pos)
    iw = iw * IDX_HEADS ** -0.5
    return aq, ak, av, bq, bk, bv, iq, ik, iw


def attn_merge(a, b, g_head, lam_init, w_out):
    B, T = a.shape[:2]
    a = rms_norm(a, g_head) * (1.0 - lam_init)
    cat = jnp.concatenate([a.reshape(B, T, -1), b.reshape(B, T, -1).astype(a.dtype)], axis=-1)
    return jnp.einsum('bte,ed->btd', cat, w_out)


def diff_attn_prompt(q, k, v, lam):
    B, T, H, _, d = q.shape
    nblk = T // Q_BLOCK
    scale = d ** -0.5
    kpos = jnp.arange(T)
    vf = v.astype(F32)
    qb = q.reshape(B, nblk, Q_BLOCK, H, 2, d).swapaxes(0, 1)

    def block(args):
        qi, i = args
        qpos = i * Q_BLOCK + jnp.arange(Q_BLOCK)
        s = jnp.einsum('bqhmd,bkhmd->bhmqk', qi, k).astype(F32) * scale
        s = jnp.where(kpos[None, :] <= qpos[:, None], s, -jnp.inf)
        p = jax.nn.softmax(s, axis=-1)
        a = p[:, :, 0] - lam * p[:, :, 1]
        return jnp.einsum('bhqk,bkhe->bqhe', a, vf)

    out = lax.map(block, (qb, jnp.arange(nblk)))
    return out.swapaxes(0, 1).reshape(B, T, H, v.shape[-1])


def diff_attn_sample(q, k, v, cache_k, cache_v, li, page_table, lam):
    Tn = q.shape[1]
    scale = DIFF_QK_DIM ** -0.5
    causal = jnp.tril(jnp.ones((Tn, Tn), dtype=bool))
    s = jnp.einsum('bqhmd,bkhmd->bhmqk', q, k).astype(F32) * scale
    s = jnp.where(causal, s, -jnp.inf)
    m = jnp.max(s, axis=-1)
    p = jnp.exp(s - m[..., None])
    l = jnp.sum(p, axis=-1)
    acc = jnp.einsum('bhmqk,bkhe->bhmqe', p, v.astype(F32))

    def step(carry, phys):
        m, l, acc = carry
        kp = cache_k[li, phys]
        vp = cache_v[li, phys]
        s = jnp.einsum('bqhmd,bkhmd->bhmqk', q, kp).astype(F32) * scale
        m_new = jnp.maximum(m, jnp.max(s, axis=-1))
        corr = jnp.exp(m - m_new)
        p = jnp.exp(s - m_new[..., None])
        l = l * corr + jnp.sum(p, axis=-1)
        acc = acc * corr[..., None] + jnp.einsum('bhmqk,bkhe->bhmqe', p, vp.astype(F32))
        return (m_new, l, acc), None

    (m, l, acc), _ = lax.scan(step, (m, l, acc), page_table.T)
    o = acc / l[..., None]
    out = o[:, :, 0] - lam * o[:, :, 1]
    return out.transpose(0, 2, 1, 3)


def indexer_scores(qi, ki, w):
    dots = jnp.einsum('bqhd,bsd->bqhs', qi, ki).astype(F32) * IDX_DIM ** -0.5
    return jnp.einsum('bqhs,bqh->bqs', jax.nn.relu(dots), w.astype(F32))


def sparse_attend(q, ks, vs, valid):
    s = jnp.einsum('bqhd,bqkhd->bqhk', q, ks).astype(F32) * DSA_HEAD_DIM ** -0.5
    s = jnp.where(valid[:, :, None, :], s, -jnp.inf)
    p = jax.nn.softmax(s, axis=-1)
    return jnp.einsum('bqhk,bqkhd->bqhd', p, vs.astype(F32))


def dsa_prompt(q, k, v, qi, ki, w):
    B, T, H, d = q.shape
    topk = min(DSA_TOPK, T // 4)
    nblk = T // DSA_Q_BLOCK
    kpos = jnp.arange(T)

    def blocks(x):
        return x.reshape((B, nblk, DSA_Q_BLOCK) + x.shape[2:]).swapaxes(0, 1)

    def block(args):
        qb, qib, wb, i = args
        qpos = i * DSA_Q_BLOCK + jnp.arange(DSA_Q_BLOCK)
        score = indexer_scores(qib, ki, wb)
        score = jnp.where(kpos[None, None, :] <= qpos[None, :, None], score, -jnp.inf)
        _, sel = lax.top_k(score, topk)
        valid = sel <= qpos[None, :, None]
        return sparse_attend(qb, gather_rows(k, sel), gather_rows(v, sel), valid)

    out = lax.map(block, (blocks(q), blocks(qi), blocks(w), jnp.arange(nblk)))
    return out.swapaxes(0, 1).reshape(B, T, H, d)


def dsa_sample(q, k, v, qi, ki, w, cache_k, cache_v, cache_ik, li, page_table):
    Bd, Tn, H, d = q.shape
    L = PAST_LEN + Tn
    topk = min(DSA_TOPK, L // 4)
    ki_past = cache_ik[li, page_table].reshape(Bd, PAST_LEN, IDX_DIM)
    ki_all = jnp.concatenate([ki_past.astype(ki.dtype), ki], axis=1)
    qpos = PAST_LEN + jnp.arange(Tn)
    score = indexer_scores(qi, ki_all, w)
    score = jnp.where(jnp.arange(L)[None, None, :] <= qpos[None, :, None], score, -jnp.inf)
    _, sel = lax.top_k(score, topk)
    valid = sel <= qpos[None, :, None]
    in_past = (sel < PAST_LEN)[..., None, None]
    ps = jnp.minimum(sel, PAST_LEN - 1)
    phys = jnp.take_along_axis(page_table, (ps // PAGE_SIZE).reshape(Bd, -1), axis=1).reshape(sel.shape)
    slot = ps % PAGE_SIZE
    new_i = jnp.clip(sel - PAST_LEN, 0, Tn - 1)
    ks = jnp.where(in_past, cache_k[li, phys, slot].astype(k.dtype), gather_rows(k, new_i))
    vs = jnp.where(in_past, cache_v[li, phys, slot].astype(v.dtype), gather_rows(v, new_i))
    return sparse_attend(q, ks, vs, valid)


def mlstm_chunkwise(q, k, v, ig, logf, c0, n0, m0):
    B, T, H, dk = q.shape
    L = math.gcd(T, MLSTM_CHUNK)
    nc = T // L

    def chunks(x):
        x = x.astype(F32).reshape((B, nc, L, H) + x.shape[3:])
        return jnp.moveaxis(jnp.moveaxis(x, 1, 0), 2, 3)

    causal = jnp.tril(jnp.ones((L, L), dtype=bool))

    def step(carry, xs):
        c, n, m = carry
        qc, kc, vc, ic, fc = xs
        b = jnp.cumsum(fc, axis=-1)
        dmat = jnp.where(causal, b[..., :, None] - b[..., None, :] + ic[..., None, :], -jnp.inf)
        inter = b + m[..., None]
        mt = jnp.maximum(inter, jnp.max(dmat, axis=-1))
        s = jnp.einsum('bhtd,bhsd->bhts', qc, kc) * jnp.exp(dmat - mt[..., None])
        w_inter = jnp.exp(inter - mt)
        num = jnp.einsum('bhts,bhse->bhte', s, vc) + w_inter[..., None] * jnp.einsum('bhed,bhtd->bhte', c, qc)
        den = jnp.sum(s, axis=-1) + w_inter * jnp.einsum('bhd,bhtd->bht', n, qc)
        h = num / jnp.maximum(jnp.abs(den), jnp.exp(-mt))[..., None]
        b_end = b[..., -1]
        d_end = b_end[..., None] - b + ic
        m_new = jnp.maximum(b_end + m, jnp.max(d_end, axis=-1))
        w_end = jnp.exp(d_end - m_new[..., None])
        decay = jnp.exp(b_end + m - m_new)
        c_new = decay[..., None, None] * c + jnp.einsum('bhs,bhse,bhsd->bhed', w_end, vc, kc)
        n_new = decay[..., None] * n + jnp.einsum('bhs,bhsd->bhd', w_end, kc)
        return (c_new, n_new, m_new), h

    xs = (chunks(q) * dk ** -0.5, chunks(k), chunks(v), chunks(ig), chunks(logf))
    (c, n, m), h = lax.scan(step, (c0.astype(F32), n0.astype(F32), m0.astype(F32)), xs)
    h = jnp.moveaxis(jnp.moveaxis(h, 3, 2), 0, 1).reshape(B, T, H, v.shape[-1])
    return h, (c, n, m)


def mlstm_mix(h, w_in, b_i, b_f, g_head, w_out, c0, n0, m0):
    B, T, _ = h.shape
    q, k, v, o, ig, fg = _split(jnp.einsum('btd,de->bte', h, w_in), ODD_SPLITS)
    ig = ig.astype(F32) + b_i.astype(F32)
    ig = IGATE_SOFTCAP * jnp.tanh(ig / IGATE_SOFTCAP)
    logf = jax.nn.log_sigmoid(fg.astype(F32) + b_f.astype(F32))
    hh, state = mlstm_chunkwise(q.reshape(B, T, MLSTM_HEADS, MLSTM_QK_DIM),
                                k.reshape(B, T, MLSTM_HEADS, MLSTM_QK_DIM),
                                v.reshape(B, T, MLSTM_HEADS, MLSTM_V_DIM), ig, logf, c0, n0, m0)
    hh = rms_norm(hh, g_head).reshape(B, T, ODD_OUT) * jax.nn.sigmoid(o.astype(F32))
    return jnp.einsum('bte,ed->btd', hh.astype(h.dtype), w_out), state


def setup_inputs(seed: int = 0) -> dict:
    key = jax.random.key(seed)
    ks = jax.random.split(key, 32)
    n_pages = PAST_LEN // PAGE_SIZE
    n_pool = (DEC_BATCH * n_pages * POOL_NUM) // POOL_DEN

    def nrm(k, shape, scale=1.0):
        return jax.random.normal(k, shape, F32) * scale

    def gain(k, shape):
        return 1.0 + 0.02 * jax.random.normal(k, shape, F32)

    perm = jax.random.permutation(ks[8], n_pool)
    page_table = perm[:DEC_BATCH * n_pages].reshape(DEC_BATCH, n_pages).astype(jnp.int32)
    NA, NM = N_ATTN_LAYERS, N_MLSTM_LAYERS
    return {
        'x_prompt': nrm(ks[0], (BATCH, SEQ, D_MODEL)),
        'x_sample': nrm(ks[1], (DEC_BATCH, DEC_SEQ, D_MODEL)),
        'cache_diff_k': nrm(ks[2], (NA, n_pool, PAGE_SIZE, DIFF_HEADS, 2, DIFF_QK_DIM)),
        'cache_diff_v': nrm(ks[3], (NA, n_pool, PAGE_SIZE, DIFF_HEADS, DIFF_V_DIM)),
        'cache_dsa_k': nrm(ks[4], (NA, n_pool, PAGE_SIZE, DSA_HEADS, DSA_HEAD_DIM)),
        'cache_dsa_v': nrm(ks[5], (NA, n_pool, PAGE_SIZE, DSA_HEADS, DSA_HEAD_DIM)),
        'cache_idx_k': nrm(ks[6], (NA, n_pool, PAGE_SIZE, IDX_DIM)),
        'state_mlstm_c': nrm(ks[7], (NM, DEC_BATCH, MLSTM_HEADS, MLSTM_V_DIM, MLSTM_QK_DIM), 0.1),
        'state_mlstm_n': nrm(ks[9], (NM, DEC_BATCH, MLSTM_HEADS, MLSTM_QK_DIM), 0.1),
        'state_mlstm_m': nrm(ks[10], (NM, DEC_BATCH, MLSTM_HEADS)),
        'page_table': page_table,
        'g_mix': gain(ks[11], (DEPTH, D_MODEL)),
        'g_ffn': gain(ks[12], (DEPTH, D_MODEL)),
        'g_final': gain(ks[13], (D_MODEL,)),
        'w_attn_in': nrm(ks[14], (NA, D_MODEL, EVEN_IN), D_MODEL ** -0.5),
        'w_attn_out': nrm(ks[15], (NA, EVEN_OUT, D_MODEL), EVEN_OUT ** -0.5),
        'lambda_q1': nrm(ks[16], (NA, DIFF_QK_DIM), 0.1),
        'lambda_k1': nrm(ks[17], (NA, DIFF_QK_DIM), 0.1),
        'lambda_q2': nrm(ks[18], (NA, DIFF_QK_DIM), 0.1),
        'lambda_k2': nrm(ks[19], (NA, DIFF_QK_DIM), 0.1),
        'g_diff_head': gain(ks[20], (NA, DIFF_V_DIM)),
        'w_mlstm_in': nrm(ks[21], (NM, D_MODEL, ODD_IN), D_MODEL ** -0.5),
        'b_mlstm_i': -2.0 + 0.1 * jax.random.normal(ks[22], (NM, MLSTM_HEADS), F32),
        'b_mlstm_f': 3.0 + 0.5 * jax.random.normal(ks[23], (NM, MLSTM_HEADS), F32),
        'g_mlstm_head': gain(ks[24], (NM, MLSTM_HEADS, MLSTM_V_DIM)),
        'w_mlstm_out': nrm(ks[25], (NM, ODD_OUT, D_MODEL), ODD_OUT ** -0.5),
        'w_ffn_up': nrm(ks[26], (DEPTH, D_MODEL, D_FF), D_MODEL ** -0.5),
        'w_ffn_down': nrm(ks[27], (DEPTH, D_FF, D_MODEL), D_FF ** -0.5),
    }


def reference(x_prompt, x_sample, cache_diff_k, cache_diff_v, cache_dsa_k, cache_dsa_v, cache_idx_k,
              state_mlstm_c, state_mlstm_n, state_mlstm_m, page_table, g_mix, g_ffn, g_final,
              w_attn_in, w_attn_out, lambda_q1, lambda_k1, lambda_q2, lambda_k2, g_diff_head,
              w_mlstm_in, b_mlstm_i, b_mlstm_f, g_mlstm_head, w_mlstm_out, w_ffn_up, w_ffn_down):
    dt = x_prompt.dtype
    B, T, _ = x_prompt.shape
    Tn = x_sample.shape[1]
    pos_p = jnp.arange(T)
    pos_s = PAST_LEN + jnp.arange(Tn)
    yp, ys = x_prompt, x_sample
    dk_p, dv_p, sk_p, sv_p, ik_p, c_p, n_p, m_p = [], [], [], [], [], [], [], []
    dk_s, dv_s, sk_s, sv_s, ik_s, c_s, n_s, m_s = [], [], [], [], [], [], [], []
    for layer in range(DEPTH):
        li = layer // 2
        hp = rms_norm(yp, g_mix[layer])
        hs = rms_norm(ys, g_mix[layer])
        if layer % 2 == 0:
            lam_init = 0.8 - 0.6 * math.exp(-0.3 * layer)
            lam = (jnp.exp(jnp.sum(lambda_q1[li].astype(F32) * lambda_k1[li].astype(F32)))
                   - jnp.exp(jnp.sum(lambda_q2[li].astype(F32) * lambda_k2[li].astype(F32))) + lam_init)
            aq, ak, av, bq, bk, bv, iq, ik, iw = attn_project(hp, pos_p, w_attn_in[li])
            a_out = diff_attn_prompt(aq, ak, av, lam)
            b_out = dsa_prompt(bq, bk, bv, iq, ik, iw)
            mix_p = attn_merge(a_out, b_out, g_diff_head[li], lam_init, w_attn_out[li])
            dk_p.append(ak); dv_p.append(av); sk_p.append(bk); sv_p.append(bv); ik_p.append(ik)
            aq, ak, av, bq, bk, bv, iq, ik, iw = attn_project(hs, pos_s, w_attn_in[li])
            a_out = diff_attn_sample(aq, ak, av, cache_diff_k, cache_diff_v, li, page_table, lam)
            b_out = dsa_sample(bq, bk, bv, iq, ik, iw, cache_dsa_k, cache_dsa_v, cache_idx_k, li, page_table)
            mix_s = attn_merge(a_out, b_out, g_diff_head[li], lam_init, w_attn_out[li])
            dk_s.append(ak); dv_s.append(av); sk_s.append(bk); sv_s.append(bv); ik_s.append(ik)
        else:
            zc = jnp.zeros((B, MLSTM_HEADS, MLSTM_V_DIM, MLSTM_QK_DIM), F32)
            zn = jnp.zeros((B, MLSTM_HEADS, MLSTM_QK_DIM), F32)
            zm = jnp.zeros((B, MLSTM_HEADS), F32)
            mix_p, (c, n, m) = mlstm_mix(hp, w_mlstm_in[li], b_mlstm_i[li], b_mlstm_f[li], g_mlstm_head[li],
                                         w_mlstm_out[li], zc, zn, zm)
            c_p.append(c); n_p.append(n); m_p.append(m)
            mix_s, (c, n, m) = mlstm_mix(hs, w_mlstm_in[li], b_mlstm_i[li], b_mlstm_f[li], g_mlstm_head[li],
                                         w_mlstm_out[li], state_mlstm_c[li], state_mlstm_n[li], state_mlstm_m[li])
            c_s.append(c); n_s.append(n); m_s.append(m)
        yp = yp + mix_p.astype(dt)
        ys = ys + mix_s.astype(dt)
        yp = yp + sqrelu_ffn(rms_norm(yp, g_ffn[layer]), w_ffn_up[layer], w_ffn_down[layer]).astype(dt)
        ys = ys + sqrelu_ffn(rms_norm(ys, g_ffn[layer]), w_ffn_up[layer], w_ffn_down[layer]).astype(dt)
    y_prompt = rms_norm(yp, g_final)
    y_sample = rms_norm(ys, g_final)
    return (y_prompt, y_sample,
            jnp.stack(dk_p), jnp.stack(dv_p), jnp.stack(sk_p), jnp.stack(sv_p), jnp.stack(ik_p),
            jnp.stack(c_p), jnp.stack(n_p), jnp.stack(m_p),
            jnp.stack(dk_s), jnp.stack(dv_s), jnp.stack(sk_s), jnp.stack(sv_s), jnp.stack(ik_s),
            jnp.stack(c_s), jnp.stack(n_s), jnp.stack(m_s))
```

```python
import functools
import math

import jax
import jax.numpy as jnp
from jax import lax
from jax.experimental import pallas as pl
from jax.experimental.pallas import tpu as pltpu

F32 = jnp.float32
BF16 = jnp.bfloat16
I32 = jnp.int32

DIFF_HEADS = 4
DIFF_QK_DIM = 64
DIFF_V_DIM = 128
DSA_HEADS = 8
DSA_HEAD_DIM = 64
IDX_HEADS = 4
IDX_DIM = 64
DSA_TOPK = 256
MLSTM_HEADS = 4
MLSTM_CHUNK = 64
IGATE_SOFTCAP = 15.0
ROPE_THETA = 10000.0
RMS_EPS = 1e-6
HEAD_W = DIFF_HEADS * 2 * DIFF_QK_DIM

LANES = 128
SUBLANES = 8
VMEM_LIMIT = 56 * 1024 * 1024

INT_MIN = -(2 ** 31)
NEG_INF = float("-inf")


def _cparams(sem):
    return pltpu.CompilerParams(dimension_semantics=sem, vmem_limit_bytes=VMEM_LIMIT)


def _rms(x, g):
    return x * lax.rsqrt(jnp.mean(x * x, axis=-1, keepdims=True) + RMS_EPS) * g


def _dot(a, b):
    return jnp.dot(a, b, preferred_element_type=F32)


def _dot_nt(a, b):
    return lax.dot_general(a, b, (((1,), (1,)), ((), ())), preferred_element_type=F32)


def _dot_tn(a, b):
    return lax.dot_general(a, b, (((0,), (0,)), ((), ())), preferred_element_type=F32)


def _norm_proj_kernel(x_ref, g_ref, w_ref, cos_ref, sin_ref, *out_refs, segs, tm):
    x = x_ref[...]
    h = _rms(x, g_ref[...]).astype(BF16)
    lane = lax.broadcasted_iota(I32, (tm, LANES), 1)
    first_half = (lane & (IDX_DIM // 2)) == 0
    for (c0, width, mode), o_ref in zip(segs, out_refs):
        y = _dot(h, w_ref[:, c0:c0 + width])
        if mode == "plain":
            o_ref[...] = y
            continue
        cos = cos_ref[...]
        sin = sin_ref[...]
        for c in range(0, width, LANES):
            yc = y[:, c:c + LANES]
            partner = jnp.where(first_half, pltpu.roll(yc, LANES - 32, 1), pltpu.roll(yc, 32, 1))
            r = yc * cos + partner * sin
            if mode == "tail":
                r = jnp.where(lane < IDX_DIM, r, yc * (IDX_HEADS ** -0.5))
            o_ref[:, c:c + LANES] = r


def _norm_proj(x, g, w, segs, cos=None, sin=None, tm=512):
    n, d = x.shape
    tm = min(tm, n)
    assert n % tm == 0
    if cos is None:
        cos = jnp.zeros((tm, LANES), F32)
        sin = cos
    p = cos.shape[0]
    assert p % tm == 0
    nper = p // tm
    kern = functools.partial(_norm_proj_kernel, segs=tuple(segs), tm=tm)
    return pl.pallas_call(
        kern,
        grid=(n // tm,),
        in_specs=[
            pl.BlockSpec((tm, d), lambda i: (i, 0)),
            pl.BlockSpec((1, d), lambda i: (0, 0)),
            pl.BlockSpec(w.shape, lambda i: (0, 0)),
            pl.BlockSpec((tm, LANES), lambda i: (i % nper, 0)),
            pl.BlockSpec((tm, LANES), lambda i: (i % nper, 0)),
        ],
        out_specs=[pl.BlockSpec((tm, wd), lambda i: (i, 0)) for (_, wd, _) in segs],
        out_shape=[jax.ShapeDtypeStruct((n, wd), F32) for (_, wd, _) in segs],
        compiler_params=_cparams(("parallel",)),
    )(x, g.reshape(1, d), w, cos, sin)


def _rope_tables(pos):
    half = IDX_DIM // 2
    inv = ROPE_THETA ** (-jnp.arange(half, dtype=F32) / half)
    ang = pos.astype(F32)[:, None] * inv[None, :]
    cos = jnp.cos(ang)
    sin = jnp.sin(ang)
    cos_t = jnp.tile(cos, (1, LANES // half))
    sin_t = jnp.tile(jnp.concatenate([-sin, sin], axis=1), (1, LANES // IDX_DIM))
    return cos_t, sin_t


def _out_proj_kernel(*refs, widths):
    n_in = len(widths)
    x_refs = refs[:n_in]
    w_ref, res_ref, o_ref = refs[n_in:]
    acc = res_ref[...]
    r0 = 0
    for x_ref, wd in zip(x_refs, widths):
        acc = acc + _dot(x_ref[...].astype(BF16), w_ref[r0:r0 + wd, :])
        r0 += wd
    o_ref[...] = acc


def _out_proj(xs, w, res, tm=512):
    n, d = res.shape
    tm = min(tm, n)
    widths = tuple(x.shape[1] for x in xs)
    kern = functools.partial(_out_proj_kernel, widths=widths)
    return pl.pallas_call(
        kern,
        grid=(n // tm,),
        in_specs=[pl.BlockSpec((tm, wd), lambda i: (i, 0)) for wd in widths]
        + [pl.BlockSpec(w.shape, lambda i: (0, 0)), pl.BlockSpec((tm, d), lambda i: (i, 0))],
        out_specs=pl.BlockSpec((tm, d), lambda i: (i, 0)),
        out_shape=jax.ShapeDtypeStruct((n, d), F32),
        compiler_params=_cparams(("parallel",)),
    )(*xs, w, res)


def _ffn_kernel(x_ref, g_ref, wu_ref, wd_ref, gf_ref, o_ref, h_scr, acc_scr, *, final_norm):
    f = pl.program_id(1)

    @pl.when(f == 0)
    def _():
        x = x_ref[...]
        h_scr[...] = _rms(x, g_ref[...]).astype(BF16)
        acc_scr[...] = x

    u = jnp.maximum(_dot(h_scr[...], wu_ref[...]), 0.0)
    acc_scr[...] += _dot((u * u).astype(BF16), wd_ref[...])

    @pl.when(f == pl.num_programs(1) - 1)
    def _():
        y = acc_scr[...]
        if final_norm:
            y = _rms(y, gf_ref[...])
        o_ref[...] = y


def _ffn(x, g, w_up, w_down, g_final=None, tm=512, tf=1024):
    n, d = x.shape
    dff = w_up.shape[1]
    tm = min(tm, n)
    final_norm = g_final is not None
    if g_final is None:
        g_final = g
    kern = functools.partial(_ffn_kernel, final_norm=final_norm)
    return pl.pallas_call(
        kern,
        grid=(n // tm, dff // tf),
        in_specs=[
            pl.BlockSpec((tm, d), lambda i, f: (i, 0)),
            pl.BlockSpec((1, d), lambda i, f: (0, 0)),
            pl.BlockSpec((d, tf), lambda i, f: (0, f)),
            pl.BlockSpec((tf, d), lambda i, f: (f, 0)),
            pl.BlockSpec((1, d), lambda i, f: (0, 0)),
        ],
        out_specs=pl.BlockSpec((tm, d), lambda i, f: (i, 0)),
        out_shape=jax.ShapeDtypeStruct((n, d), F32),
        scratch_shapes=[pltpu.VMEM((tm, d), BF16), pltpu.VMEM((tm, d), F32)],
        compiler_params=_cparams(("parallel", "arbitrary")),
    )(x, g.reshape(1, d), w_up, w_down, g_final.reshape(1, d))


def _lambda(lam_ref, lam_init):
    l = lam_ref[...]
    a = jnp.sum(l[0:1] * l[1:2], axis=1, keepdims=True)
    b = jnp.sum(l[2:3] * l[3:4], axis=1, keepdims=True)
    return jnp.exp(a) - jnp.exp(b) + lam_init


def _split_halves(q):
    lane = lax.broadcasted_iota(I32, q.shape, 1)
    lo = lane < DSA_HEAD_DIM
    return jnp.concatenate([jnp.where(lo, q, 0.0), jnp.where(lo, 0.0, q)], axis=0)


def _sortable_key(score):
    bits = pltpu.bitcast(score, I32)
    return jnp.where(bits >= 0, bits, bits ^ jnp.int32(0x7FFFFFFF))


def _kth_largest(count_ge, rows, kk):
    def body(it, t):
        cand = t + jnp.left_shift(jnp.int32(1), 31 - it)
        return jnp.where(count_ge(cand) >= kk, cand, t)

    return lax.fori_loop(0, 32, body, jnp.full((rows, 1), INT_MIN, I32))


def _diff_attn_kernel(lam_ref, g_ref, q_ref, k_ref, v_ref, o_ref, m_scr, l_scr, acc_scr, *, blk, lam_init):
    i = pl.program_id(2)
    q2 = _split_halves(q_ref[0] * (DIFF_QK_DIM ** -0.5)).astype(BF16)
    m_scr[...] = jnp.full(m_scr.shape, NEG_INF, F32)
    l_scr[...] = jnp.zeros(l_scr.shape, F32)
    acc_scr[...] = jnp.zeros(acc_scr.shape, F32)
    row = lax.broadcasted_iota(I32, (2 * blk, blk), 0)
    qpos = i * blk + jnp.where(row >= blk, row - blk, row)
    col = lax.broadcasted_iota(I32, (2 * blk, blk), 1)

    def body(j, carry):
        start = pl.multiple_of(j * blk, blk)
        kj = k_ref[0, pl.ds(start, blk), :].astype(BF16)
        vj = v_ref[0, pl.ds(start, blk), :].astype(BF16)
        s = _dot_nt(q2, kj)
        s = jnp.where(j * blk + col <= qpos, s, NEG_INF)
        m_old = m_scr[...]
        m_new = jnp.maximum(m_old, jnp.max(s, axis=1, keepdims=True))
        corr = jnp.exp(m_old - m_new)
        p = jnp.exp(s - m_new)
        l_scr[...] = l_scr[...] * corr + jnp.sum(p, axis=1, keepdims=True)
        acc_scr[...] = acc_scr[...] * corr + _dot(p.astype(BF16), vj)
        m_scr[...] = m_new
        return carry

    lax.fori_loop(0, i + 1, body, 0)
    o = acc_scr[...] / l_scr[...]
    a = o[:blk] - _lambda(lam_ref, lam_init) * o[blk:]
    o_ref[0] = _rms(a, g_ref[...]) * (1.0 - lam_init)


def _diff_attn_prompt(q, k, v, lam_rows, g_head, lam_init, blk=256):
    b, t, _ = q.shape
    blk = min(blk, t)
    kern = functools.partial(_diff_attn_kernel, blk=blk, lam_init=lam_init)
    return pl.pallas_call(
        kern,
        grid=(b, DIFF_HEADS, t // blk),
        in_specs=[
            pl.BlockSpec(lam_rows.shape, lambda bi, h, i: (0, 0)),
            pl.BlockSpec((1, DIFF_V_DIM), lambda bi, h, i: (0, 0)),
            pl.BlockSpec((1, blk, LANES), lambda bi, h, i: (bi, i, h)),
            pl.BlockSpec((1, t, LANES), lambda bi, h, i: (bi, 0, h)),
            pl.BlockSpec((1, t, LANES), lambda bi, h, i: (bi, 0, h)),
        ],
        out_specs=pl.BlockSpec((1, blk, LANES), lambda bi, h, i: (bi, i, h)),
        out_shape=jax.ShapeDtypeStruct((b, t, HEAD_W), F32),
        scratch_shapes=[pltpu.VMEM((2 * blk, 1), F32), pltpu.VMEM((2 * blk, 1), F32),
                        pltpu.VMEM((2 * blk, LANES), F32)],
        compiler_params=_cparams(("parallel", "parallel", "parallel")),
    )(lam_rows, g_head.reshape(1, DIFF_V_DIM), q, k, v)


def _dsa_prompt_kernel(q_ref, iq_ref, tail_q_ref, k_ref, v_ref, tail_k_ref, o_ref, key_scr, bias_scr,
                       *, qb, t, topk):
    i = pl.program_id(1)
    qpos = i * qb + lax.broadcasted_iota(I32, (qb, t), 0)
    kpos = lax.broadcasted_iota(I32, (qb, t), 1)
    causal = kpos <= qpos

    ik = tail_k_ref[0, :, :IDX_DIM].astype(BF16)
    tq = tail_q_ref[0]
    score = jnp.zeros((qb, t), F32)
    for h in range(IDX_HEADS):
        qh = iq_ref[0, :, h * IDX_DIM:(h + 1) * IDX_DIM].astype(BF16)
        dots = _dot_nt(qh, ik) * (IDX_DIM ** -0.5)
        score = score + jnp.maximum(dots, 0.0) * tq[:, IDX_DIM + h:IDX_DIM + h + 1]
    score = score + 0.0
    key_scr[...] = jnp.where(causal, _sortable_key(score), INT_MIN)

    def count_ge(cand):
        return jnp.sum(jnp.where(key_scr[...] >= cand, 1.0, 0.0), axis=1, keepdims=True)

    thr = _kth_largest(count_ge, qb, float(topk))
    key = key_scr[...]
    above = key > thr
    n_above = jnp.sum(jnp.where(above, 1.0, 0.0), axis=1, keepdims=True)
    room = float(topk) - n_above
    tie = key == thr
    tri = (lax.broadcasted_iota(I32, (LANES, LANES), 0) <= lax.broadcasted_iota(I32, (LANES, LANES), 1))
    tri = jnp.where(tri, 1.0, 0.0).astype(BF16)
    before = jnp.zeros((qb, 1), F32)
    for c in range(0, t, LANES):
        tc = tie[:, c:c + LANES]
        rank = _dot(jnp.where(tc, 1.0, 0.0).astype(BF16), tri) + before
        sel = (above[:, c:c + LANES] | (tc & (rank <= room))) & causal[:, c:c + LANES]
        bias_scr[:, c:c + LANES] = jnp.where(sel, 0.0, NEG_INF)
        before = rank[:, LANES - 1:LANES]

    lane = lax.broadcasted_iota(I32, (qb, LANES), 1)
    for pb in range(DSA_HEADS // 2):
        cols = slice(pb * LANES, (pb + 1) * LANES)
        q2 = _split_halves(q_ref[0, :, cols] * (DSA_HEAD_DIM ** -0.5)).astype(BF16)
        kp = k_ref[0, :, cols].astype(BF16)
        vp = v_ref[0, :, cols].astype(BF16)
        outs = []
        for m in range(2):
            s = _dot_nt(q2[m * qb:(m + 1) * qb], kp) + bias_scr[...]
            mx = jnp.max(s, axis=1, keepdims=True)
            p = jnp.exp(s - mx)
            den = jnp.sum(p, axis=1, keepdims=True)
            outs.append(_dot(p.astype(BF16), vp) / den)
        o_ref[0, :, cols] = jnp.where(lane < DSA_HEAD_DIM, outs[0], outs[1])


def _dsa_prompt(q, k, v, iq, tail, qb=256):
    b, t, _ = q.shape
    qb = min(qb, t)
    topk = min(DSA_TOPK, t // 4)
    kern = functools.partial(_dsa_prompt_kernel, qb=qb, t=t, topk=topk)
    return pl.pallas_call(
        kern,
        grid=(b, t // qb),
        in_specs=[
            pl.BlockSpec((1, qb, HEAD_W), lambda bi, i: (bi, i, 0)),
            pl.BlockSpec((1, qb, IDX_HEADS * IDX_DIM), lambda bi, i: (bi, i, 0)),
            pl.BlockSpec((1, qb, LANES), lambda bi, i: (bi, i, 0)),
            pl.BlockSpec((1, t, HEAD_W), lambda bi, i: (bi, 0, 0)),
            pl.BlockSpec((1, t, HEAD_W), lambda bi, i: (bi, 0, 0)),
            pl.BlockSpec((1, t, LANES), lambda bi, i: (bi, 0, 0)),
        ],
        out_specs=pl.BlockSpec((1, qb, HEAD_W), lambda bi, i: (bi, i, 0)),
        out_shape=jax.ShapeDtypeStruct((b, t, HEAD_W), F32),
        scratch_shapes=[pltpu.VMEM((qb, t), I32), pltpu.VMEM((qb, t), F32)],
        compiler_params=_cparams(("parallel", "parallel")),
    )(q, iq, tail, k, v, tail)


def _dsa_sample_select_kernel(pt_ref, iq_ref, tail_ref, *refs, pg, tn, n_chunks, topk):
    page_refs = refs[:pg]
    bias_ref = refs[pg]
    key_scr = refs[pg + 1]
    j = pl.program_id(1)
    tail = tail_ref[0]
    iq = iq_ref[0]
    q_rows = jnp.concatenate([iq[:, h * IDX_DIM:(h + 1) * IDX_DIM] for h in range(IDX_HEADS)], axis=0).astype(BF16)
    w_rows = jnp.concatenate([tail[:, IDX_DIM + h:IDX_DIM + h + 1] for h in range(IDX_HEADS)], axis=0)

    def scores(dots):
        d = jnp.maximum(dots * (IDX_DIM ** -0.5), 0.0) * w_rows
        s = d[0:tn]
        for h in range(1, IDX_HEADS):
            s = s + d[h * tn:(h + 1) * tn]
        return s + 0.0

    for g in range(pg):
        s = scores(_dot(q_rows, page_refs[g][0, 0].astype(BF16)))
        key_scr[j * pg + g] = _sortable_key(s)

    @pl.when(j == pl.num_programs(1) - 1)
    def _():
        n_past = n_chunks - 1
        k_new = jnp.concatenate([tail[:, :IDX_DIM], jnp.zeros((LANES - tn, IDX_DIM), F32)], axis=0).astype(BF16)
        s_new = scores(_dot_nt(q_rows, k_new))
        row = lax.broadcasted_iota(I32, (tn, LANES), 0)
        col = lax.broadcasted_iota(I32, (tn, LANES), 1)
        key_scr[n_past] = jnp.where(col <= row, _sortable_key(s_new), INT_MIN)

        def count_ge(cand):
            hit = jnp.where(key_scr[...] >= cand[None], 1.0, 0.0)
            return jnp.sum(jnp.sum(hit, axis=0), axis=1, keepdims=True)

        thr = _kth_largest(count_ge, tn, float(topk))
        key = key_scr[...]
        above = key > thr[None]
        n_above = jnp.sum(jnp.sum(jnp.where(above, 1.0, 0.0), axis=0), axis=1, keepdims=True)
        room = float(topk) - n_above
        tie = jnp.where(key == thr[None], 1.0, 0.0)
        tri = (lax.broadcasted_iota(I32, (LANES, LANES), 0) <= lax.broadcasted_iota(I32, (LANES, LANES), 1))
        tri = jnp.where(tri, 1.0, 0.0).astype(BF16)
        within = _dot(tie.reshape(n_chunks * tn, LANES).astype(BF16), tri).reshape(n_chunks, tn, LANES)
        totals = within[:, :, LANES - 1:LANES]

        def chunk_body(c, before):
            rank = within[c] + before
            sel = above[c] | ((tie[c] > 0) & (rank <= room))
            bias_ref[0, c] = jnp.where(sel, 0.0, NEG_INF)
            return before + totals[c]

        before = jnp.zeros((tn, 1), F32)
        for c in range(n_chunks):
            before = chunk_body(c, before)
        bias_ref[0, n_past] = jnp.where(col <= row, bias_ref[0, n_past], NEG_INF)


def _dsa_sample_select(iq, tail, cache_ik_t, page_table, li, pg=16):
    bd, tn, _ = iq.shape
    n_pages = page_table.shape[1]
    page = cache_ik_t.shape[3]
    assert page == LANES and n_pages % pg == 0
    n_chunks = n_pages + 1
    topk = min(DSA_TOPK, (n_pages * page + tn) // 4)
    kern = functools.partial(_dsa_sample_select_kernel, pg=pg, tn=tn, n_chunks=n_chunks, topk=topk)

    def page_spec(g):
        return pl.BlockSpec((1, 1, IDX_DIM, page), lambda b, j, pt: (li, pt[b, j * pg + g], 0, 0))

    return pl.pallas_call(
        kern,
        grid_spec=pltpu.PrefetchScalarGridSpec(
            num_scalar_prefetch=1,
            grid=(bd, n_pages // pg),
            in_specs=[
                pl.BlockSpec((1, tn, IDX_HEADS * IDX_DIM), lambda b, j, pt: (b, 0, 0)),
                pl.BlockSpec((1, tn, LANES), lambda b, j, pt: (b, 0, 0)),
            ] + [page_spec(g) for g in range(pg)],
            out_specs=pl.BlockSpec((1, n_chunks, tn, LANES), lambda b, j, pt: (b, 0, 0, 0)),
            scratch_shapes=[pltpu.VMEM((n_chunks, tn, LANES), I32)],
        ),
        out_shape=jax.ShapeDtypeStruct((bd, n_chunks, tn, LANES), F32),
        compiler_params=_cparams(("parallel", "arbitrary")),
    )(page_table, iq, tail, *([cache_ik_t] * pg))


def _paged_attn_kernel(pt_ref, lam_ref, g_ref, q_ref, kn_ref, vn_ref, bias_ref, *refs, pg, tn, diff, lam_init):
    k_refs = refs[:pg]
    v_refs = refs[pg:2 * pg]
    o_ref = refs[2 * pg]
    m_scr, l_scr, acc_scr = refs[2 * pg + 1:]
    j = pl.program_id(1)
    nblk = HEAD_W // LANES
    rows = 2 * tn
    scale = DIFF_QK_DIM ** -0.5
    row = lax.broadcasted_iota(I32, (rows, tn), 0)
    tq = jnp.where(row >= tn, row - tn, row)
    col = lax.broadcasted_iota(I32, (rows, tn), 1)

    def update(blk, s, v_bf16, v_transposed=False):
        m_old = m_scr[blk]
        m_new = jnp.maximum(m_old, jnp.max(s, axis=1, keepdims=True))
        corr = jnp.exp(m_old - m_new)
        p = jnp.exp(s - m_new)
        l_scr[blk] = l_scr[blk] * corr + jnp.sum(p, axis=1, keepdims=True)
        pv = _dot_nt(p.astype(BF16), v_bf16) if v_transposed else _dot(p.astype(BF16), v_bf16)
        acc_scr[blk] = acc_scr[blk] * corr + pv
        m_scr[blk] = m_new

    q_all = q_ref[0] * scale
    q2 = [_split_halves(q_all[:, b * LANES:(b + 1) * LANES]).astype(BF16) for b in range(nblk)]

    @pl.when(j == 0)
    def _():
        m_scr[...] = jnp.full(m_scr.shape, jnp.finfo(F32).min, F32)
        l_scr[...] = jnp.zeros(l_scr.shape, F32)
        acc_scr[...] = jnp.zeros(acc_scr.shape, F32)
        n_past = bias_ref.shape[1] - 1
        bn = bias_ref[0, n_past, :, :tn]
        bn2 = jnp.concatenate([bn, bn], axis=0)
        for blk in range(nblk):
            cols = slice(blk * LANES, (blk + 1) * LANES)
            s = _dot_nt(q2[blk], kn_ref[0, :, cols].astype(BF16))
            s = jnp.where(col <= tq, s, NEG_INF)
            if not diff:
                s = s + bn2
            update(blk, s, vn_ref[0, :, cols].astype(BF16))

    for g in range(pg):
        if not diff:
            bp = bias_ref[0, j * pg + g]
            bp2 = jnp.concatenate([bp, bp], axis=0)
        for blk in range(nblk):
            rows_blk = slice(blk * LANES, (blk + 1) * LANES)
            s = _dot(q2[blk], k_refs[g][0, 0, rows_blk, :].astype(BF16))
            if diff:
                update(blk, s, v_refs[g][0, 0, pl.ds(blk, LANES, stride=nblk), :].astype(BF16))
            else:
                update(blk, s + bp2, v_refs[g][0, 0, rows_blk, :].astype(BF16), v_transposed=True)

    @pl.when(j == pl.num_programs(1) - 1)
    def _():
        lane = lax.broadcasted_iota(I32, (tn, LANES), 1)
        for blk in range(nblk):
            o = acc_scr[blk] / l_scr[blk]
            if diff:
                a = o[:tn] - _lambda(lam_ref, lam_init) * o[tn:]
                r = _rms(a, g_ref[...]) * (1.0 - lam_init)
            else:
                r = jnp.where(lane < DSA_HEAD_DIM, o[:tn], o[tn:])
            o_ref[0, :, blk * LANES:(blk + 1) * LANES] = r


def _paged_attn(q, k_new, v_new, bias, cache_k, cache_v, page_table, li, lam_rows, g_head, lam_init, diff, pg=8):
    bd, tn, _ = q.shape
    n_pages = page_table.shape[1]
    assert n_pages % pg == 0 and cache_k.shape[2:] == (HEAD_W, LANES) and cache_v.shape[2:] == (HEAD_W, LANES)
    n_chunks = n_pages + 1
    kern = functools.partial(_paged_attn_kernel, pg=pg, tn=tn, diff=diff, lam_init=lam_init)
    nblk = HEAD_W // LANES

    def page_spec(g):
        return pl.BlockSpec((1, 1, HEAD_W, LANES), lambda b, j, pt: (li, pt[b, j * pg + g], 0, 0))

    tok = pl.BlockSpec((1, tn, HEAD_W), lambda b, j, pt: (b, 0, 0))
    return pl.pallas_call(
        kern,
        grid_spec=pltpu.PrefetchScalarGridSpec(
            num_scalar_prefetch=1,
            grid=(bd, n_pages // pg),
            in_specs=[
                pl.BlockSpec(lam_rows.shape, lambda b, j, pt: (0, 0)),
                pl.BlockSpec((1, DIFF_V_DIM), lambda b, j, pt: (0, 0)),
                tok, tok, tok,
                pl.BlockSpec((1, n_chunks, tn, LANES), lambda b, j, pt: (b, 0, 0, 0)),
            ] + [page_spec(g) for g in range(pg)] + [page_spec(g) for g in range(pg)],
            out_specs=pl.BlockSpec((1, tn, HEAD_W), lambda b, j, pt: (b, 0, 0)),
            scratch_shapes=[pltpu.VMEM((nblk, 2 * tn, 1), F32), pltpu.VMEM((nblk, 2 * tn, 1), F32),
                            pltpu.VMEM((nblk, 2 * tn, LANES), F32)],
        ),
        out_shape=jax.ShapeDtypeStruct((bd, tn, HEAD_W), F32),
        compiler_params=_cparams(("parallel", "arbitrary")),
    )(page_table, lam_rows, g_head.reshape(1, DIFF_V_DIM), q, k_new, v_new, bias,
      *([cache_k] * pg), *([cache_v] * pg))


def _mlstm_kernel(q_ref, k_ref, v_ref, og_ref, gcol_ref, grow_ref, bcol_ref, brow_ref, gh_ref, c0_ref, n0_ref, m0_ref,
                  h_ref, c_out, n_out, m_out, c_scr, n_scr, m_scr, *, chunk, n_sub, dk, dv):
    j = pl.program_id(1)
    nh = MLSTM_HEADS

    @pl.when(j == 0)
    def _():
        c_scr[...] = c0_ref[0]
        for h in range(nh):
            n_scr[h] = n0_ref[0, h:h + 1, :]
            m_scr[h] = m0_ref[0, :, h:h + 1]

    r_i = lax.broadcasted_iota(I32, (chunk, chunk), 0)
    c_i = lax.broadcasted_iota(I32, (chunk, chunk), 1)
    causal = c_i <= r_i

    def gates(x, is_input_gate):
        ig = IGATE_SOFTCAP * jnp.tanh(x / IGATE_SOFTCAP)
        lf = jnp.minimum(x, 0.0) - jnp.log1p(jnp.exp(-jnp.abs(x)))
        return jnp.where(is_input_gate, ig, lf)

    def chunk_body(s, carry):
        r0 = pl.multiple_of(s * chunk, chunk)
        gc = gcol_ref[0, s] + bcol_ref[...]
        gr = grow_ref[0, s] + brow_ref[...]
        gc = gates(gc, lax.broadcasted_iota(I32, gc.shape, 1) < nh)
        gr = gates(gr, lax.broadcasted_iota(I32, gr.shape, 0) < nh)
        for h in range(nh):
            i_col, f_col = gc[:, h:h + 1], gc[:, nh + h:nh + h + 1]
            i_row, f_row = gr[h:h + 1, :], gr[nh + h:nh + h + 1, :]
            b_col = jnp.sum(jnp.where(causal, f_row, 0.0), axis=1, keepdims=True)
            b_row = jnp.sum(jnp.where(r_i <= c_i, f_col, 0.0), axis=0, keepdims=True)
            m_prev = m_scr[h]
            c_prev = c_scr[h]
            n_prev = n_scr[h]
            qc = q_ref[0, pl.ds(r0, chunk), h * dk:(h + 1) * dk] * (dk ** -0.5)
            kc = k_ref[0, pl.ds(r0, chunk), h * dk:(h + 1) * dk]
            vc = v_ref[0, pl.ds(r0, chunk), h * dv:(h + 1) * dv]
            dmat = jnp.where(causal, b_col - b_row + i_row, NEG_INF)
            inter = b_col + m_prev
            mt = jnp.maximum(inter, jnp.max(dmat, axis=1, keepdims=True))
            sm = _dot_nt(qc.astype(BF16), kc.astype(BF16)) * jnp.exp(dmat - mt)
            w_inter = jnp.exp(inter - mt)
            num = _dot(sm.astype(BF16), vc.astype(BF16)) + w_inter * _dot_nt(qc.astype(BF16), c_prev.astype(BF16))
            den = jnp.sum(sm, axis=1, keepdims=True) + w_inter * jnp.sum(qc * n_prev, axis=1, keepdims=True)
            hh = num / jnp.maximum(jnp.abs(den), jnp.exp(-mt))
            b_end = jnp.sum(f_row, axis=1, keepdims=True)
            d_end = b_end - b_col + i_col
            m_new = jnp.maximum(b_end + m_prev, jnp.max(d_end, axis=0, keepdims=True))
            w_end = jnp.exp(d_end - m_new)
            decay = jnp.exp(b_end + m_prev - m_new)
            kw = kc * w_end
            c_scr[h] = decay * c_prev + _dot_tn(vc.astype(BF16), kw.astype(BF16))
            n_scr[h] = decay * n_prev + jnp.sum(kw, axis=0, keepdims=True)
            m_scr[h] = m_new
            hn = _rms(hh, gh_ref[:, h * dv:(h + 1) * dv])
            og = og_ref[0, pl.ds(r0, chunk), h * dv:(h + 1) * dv]
            h_ref[0, pl.ds(r0, chunk), h * dv:(h + 1) * dv] = hn * (1.0 / (1.0 + jnp.exp(-og)))
        return carry

    lax.fori_loop(0, n_sub, chunk_body, 0)

    @pl.when(j == pl.num_programs(1) - 1)
    def _():
        c_out[0] = c_scr[...]
        for h in range(nh):
            n_out[0, h:h + 1, :] = n_scr[h]
            m_out[0, :, h:h + 1] = m_scr[h]


def _mlstm(q, k, v, og, gates_raw, b_i, b_f, g_head, c0, n0, m0, tb=512):
    b, t, _ = q.shape
    nh = MLSTM_HEADS
    dk = q.shape[2] // nh
    dv = v.shape[2] // nh
    chunk = math.gcd(t, MLSTM_CHUNK)
    tb = min(tb, t)
    n_sub = tb // chunk
    nc = t // chunk
    g = gates_raw[:, :, :2 * nh].reshape(b, nc, chunk, 2 * nh)
    g_row = jnp.swapaxes(g, 2, 3)
    bias = jnp.concatenate([b_i, b_f]).astype(F32)
    kern = functools.partial(_mlstm_kernel, chunk=chunk, n_sub=n_sub, dk=dk, dv=dv)
    tok = lambda w: pl.BlockSpec((1, tb, w), lambda bi, j: (bi, j, 0))
    outs = pl.pallas_call(
        kern,
        grid=(b, t // tb),
        in_specs=[
            tok(nh * dk), tok(nh * dk), tok(nh * dv), tok(nh * dv),
            pl.BlockSpec((1, n_sub, chunk, 2 * nh), lambda bi, j: (bi, j, 0, 0)),
            pl.BlockSpec((1, n_sub, 2 * nh, chunk), lambda bi, j: (bi, j, 0, 0)),
            pl.BlockSpec((1, 2 * nh), lambda bi, j: (0, 0)),
            pl.BlockSpec((2 * nh, 1), lambda bi, j: (0, 0)),
            pl.BlockSpec((1, nh * dv), lambda bi, j: (0, 0)),
            pl.BlockSpec((1, nh, dv, dk), lambda bi, j: (bi, 0, 0, 0)),
            pl.BlockSpec((1, nh, dk), lambda bi, j: (bi, 0, 0)),
            pl.BlockSpec((1, 1, nh), lambda bi, j: (bi, 0, 0)),
        ],
        out_specs=[
            tok(nh * dv),
            pl.BlockSpec((1, nh, dv, dk), lambda bi, j: (bi, 0, 0, 0)),
            pl.BlockSpec((1, nh, dk), lambda bi, j: (bi, 0, 0)),
            pl.BlockSpec((1, 1, nh), lambda bi, j: (bi, 0, 0)),
        ],
        out_shape=[
            jax.ShapeDtypeStruct((b, t, nh * dv), F32),
            jax.ShapeDtypeStruct((b, nh, dv, dk), F32),
            jax.ShapeDtypeStruct((b, nh, dk), F32),
            jax.ShapeDtypeStruct((b, 1, nh), F32),
        ],
        scratch_shapes=[pltpu.VMEM((nh, dv, dk), F32), pltpu.VMEM((nh, 1, dk), F32), pltpu.VMEM((nh, 1, 1), F32)],
        compiler_params=_cparams(("parallel", "arbitrary")),
    )(q, k, v, og, g, g_row, bias.reshape(1, 2 * nh), bias.reshape(2 * nh, 1), g_head.reshape(1, nh * dv),
      c0, n0, m0.reshape(b, 1, nh))
    hh, c, n, m = outs
    return hh, c, n, m.reshape(b, nh)


_EVEN_SEGS = tuple([(i * HEAD_W, HEAD_W, mode) for i, mode in enumerate(("rope", "rope", "plain", "rope", "rope", "plain"))]
                   + [(6 * HEAD_W, IDX_HEADS * IDX_DIM, "rope"), (6 * HEAD_W + IDX_HEADS * IDX_DIM, LANES, "tail")])


def _pad_cols(w, mult=LANES):
    pad = (-w.shape[1]) % mult
    return jnp.pad(w, ((0, 0), (0, pad))) if pad else w


def _attn_layer(yp, ys, w_in, w_out, lam_rows, g_head, g_mix, lam_init, caches, page_table, li, dims):
    b, t, bd, tn, past = dims
    d = yp.shape[1]
    w_in = _pad_cols(w_in.astype(BF16))
    w_out = w_out.astype(BF16)
    cache_dk, cache_dv, cache_sk, cache_sv, cache_ik = caches

    cos, sin = _rope_tables(jnp.arange(t))
    aq, ak, av, bq, bk, bv, iq, tail = _norm_proj(yp, g_mix, w_in, _EVEN_SEGS, cos, sin)
    r3 = lambda x: x.reshape(b, t, x.shape[-1])
    a_out = _diff_attn_prompt(r3(aq), r3(ak), r3(av), lam_rows, g_head, lam_init)
    b_out = _dsa_prompt(r3(bq), r3(bk), r3(bv), r3(iq), r3(tail))
    yp = _out_proj([a_out.reshape(b * t, HEAD_W), b_out.reshape(b * t, HEAD_W)], w_out, yp)
    new_p = (ak.reshape(b, t, DIFF_HEADS, 2, DIFF_QK_DIM), av.reshape(b, t, DIFF_HEADS, DIFF_V_DIM),
             bk.reshape(b, t, DSA_HEADS, DSA_HEAD_DIM), bv.reshape(b, t, DSA_HEADS, DSA_HEAD_DIM),
             tail.reshape(b, t, LANES)[:, :, :IDX_DIM])

    cos, sin = _rope_tables(jnp.tile(past + jnp.arange(tn), bd))
    aq, ak, av, bq, bk, bv, iq, tail = _norm_proj(ys, g_mix, w_in, _EVEN_SEGS, cos, sin)
    s3 = lambda x: x.reshape(bd, tn, x.shape[-1])
    npool, page = cache_dk.shape[1], cache_dk.shape[2]
    slot_last = lambda c: jnp.moveaxis(c, 2, -1).reshape(c.shape[0], npool, -1, page)
    bias = _dsa_sample_select(s3(iq), s3(tail), slot_last(cache_ik), page_table, li)
    dv_rows = cache_dv.reshape(cache_dv.shape[0], npool, page * DIFF_HEADS, DIFF_V_DIM)
    a_out = _paged_attn(s3(aq), s3(ak), s3(av), bias, slot_last(cache_dk), dv_rows, page_table, li,
                        lam_rows, g_head, lam_init, diff=True)
    b_out = _paged_attn(s3(bq), s3(bk), s3(bv), bias, slot_last(cache_sk), slot_last(cache_sv), page_table, li,
                        lam_rows, g_head, lam_init, diff=False)
    ys = _out_proj([a_out.reshape(bd * tn, HEAD_W), b_out.reshape(bd * tn, HEAD_W)], w_out, ys)
    new_s = (ak.reshape(bd, tn, DIFF_HEADS, 2, DIFF_QK_DIM), av.reshape(bd, tn, DIFF_HEADS, DIFF_V_DIM),
             bk.reshape(bd, tn, DSA_HEADS, DSA_HEAD_DIM), bv.reshape(bd, tn, DSA_HEADS, DSA_HEAD_DIM),
             tail.reshape(bd, tn, LANES)[:, :, :IDX_DIM])
    return yp, ys, new_p, new_s


def _mlstm_layer(y, bsz, t, w_in, w_out, b_i, b_f, g_head, g_mix, c0, n0, m0):
    d = y.shape[1]
    nh = MLSTM_HEADS
    dk = d // (2 * nh)
    dv = d // nh
    segs = ((0, nh * dk, "plain"), (nh * dk, nh * dk, "plain"), (2 * nh * dk, nh * dv, "plain"),
            (2 * nh * dk + nh * dv, nh * dv, "plain"), (2 * nh * dk + 2 * nh * dv, LANES, "plain"))
    q, k, v, og, gts = _norm_proj(y, g_mix, w_in, segs)
    r3 = lambda x: x.reshape(bsz, t, x.shape[-1])
    hh, c, n, m = _mlstm(r3(q), r3(k), r3(v), r3(og), r3(gts), b_i, b_f, g_head, c0, n0, m0)
    y = _out_proj([hh.reshape(bsz * t, nh * dv)], w_out, y)
    return y, (c, n, m)


def kernel(x_prompt, x_sample, cache_diff_k, cache_diff_v, cache_dsa_k, cache_dsa_v, cache_idx_k, state_mlstm_c, state_mlstm_n, state_mlstm_m, page_table, g_mix, g_ffn, g_final, w_attn_in, w_attn_out, lambda_q1, lambda_k1, lambda_q2, lambda_k2, g_diff_head, w_mlstm_in, b_mlstm_i, b_mlstm_f, g_mlstm_head, w_mlstm_out, w_ffn_up, w_ffn_down):
    b, t, d = x_prompt.shape
    bd, tn, _ = x_sample.shape
    depth = g_mix.shape[0]
    past = page_table.shape[1] * cache_idx_k.shape[2]
    nh = MLSTM_HEADS
    yp = x_prompt.reshape(b * t, d)
    ys = x_sample.reshape(bd * tn, d)
    new_p, new_s, st_p, st_s = [], [], [], []
    for layer in range(depth):
        li = layer // 2
        last = layer == depth - 1
        if layer % 2 == 0:
            lam_init = 0.8 - 0.6 * math.exp(-0.3 * layer)
            lam_rows = jnp.stack([lambda_q1[li], lambda_k1[li], lambda_q2[li], lambda_k2[li]]).astype(F32)
            yp, ys, np_, ns_ = _attn_layer(
                yp, ys, w_attn_in[li], w_attn_out[li], lam_rows, g_diff_head[li], g_mix[layer], lam_init,
                (cache_diff_k, cache_diff_v, cache_dsa_k, cache_dsa_v, cache_idx_k), page_table, li,
                (b, t, bd, tn, past))
            new_p.append(np_)
            new_s.append(ns_)
        else:
            w_in = _pad_cols(w_mlstm_in[li].astype(BF16))
            w_out = w_mlstm_out[li].astype(BF16)
            dk = d // (2 * nh)
            dv = d // nh
            zc = jnp.zeros((b, nh, dv, dk), F32)
            zn = jnp.zeros((b, nh, dk), F32)
            zm = jnp.zeros((b, nh), F32)
            yp, sp = _mlstm_layer(yp, b, t, w_in, w_out, b_mlstm_i[li], b_mlstm_f[li], g_mlstm_head[li],
                                  g_mix[layer], zc, zn, zm)
            ys, ss = _mlstm_layer(ys, bd, tn, w_in, w_out, b_mlstm_i[li], b_mlstm_f[li], g_mlstm_head[li],
                                  g_mix[layer], state_mlstm_c[li], state_mlstm_n[li], state_mlstm_m[li])
            st_p.append(sp)
            st_s.append(ss)
        w_up = w_ffn_up[layer].astype(BF16)
        w_down = w_ffn_down[layer].astype(BF16)
        gf = g_final if last else None
        yp = _ffn(yp, g_ffn[layer], w_up, w_down, gf)
        ys = _ffn(ys, g_ffn[layer], w_up, w_down, gf)
    stack = lambda items, idx: jnp.stack([it[idx] for it in items])
    return (yp.reshape(b, t, d), ys.reshape(bd, tn, d),
            stack(new_p, 0), stack(new_p, 1), stack(new_p, 2), stack(new_p, 3), stack(new_p, 4),
            stack(st_p, 0), stack(st_p, 1), stack(st_p, 2),
            stack(new_s, 0), stack(new_s, 1), stack(new_s, 2), stack(new_s, 3), stack(new_s, 4),
            stack(st_s, 0), stack(st_s, 1), stack(st_s, 2))
```

```python
import functools
import math

import jax
import jax.numpy as jnp
from jax import lax
from jax.experimental import pallas as pl
from jax.experimental.pallas import tpu as pltpu

F32 = jnp.float32
BF16 = jnp.bfloat16
I32 = jnp.int32

DIFF_HEADS = 4
DIFF_QK_DIM = 64
DIFF_V_DIM = 128
DSA_HEADS = 8
DSA_HEAD_DIM = 64
IDX_HEADS = 4
IDX_DIM = 64
DSA_TOPK = 256
MLSTM_HEADS = 4
MLSTM_CHUNK = 64
IGATE_SOFTCAP = 15.0
ROPE_THETA = 10000.0
RMS_EPS = 1e-6
HEAD_W = DIFF_HEADS * 2 * DIFF_QK_DIM

LANES = 128
SUBLANES = 8
VMEM_LIMIT = 56 * 1024 * 1024

INT_MIN = -(2 ** 31)
NEG_INF = float("-inf")


def _cparams(sem):
    return pltpu.CompilerParams(dimension_semantics=sem, vmem_limit_bytes=VMEM_LIMIT)


def _rms(x, g):
    return x * lax.rsqrt(jnp.mean(x * x, axis=-1, keepdims=True) + RMS_EPS) * g


def _dot(a, b):
    return jnp.dot(a, b, preferred_element_type=F32)


def _dot_nt(a, b):
    return lax.dot_general(a, b, (((1,), (1,)), ((), ())), preferred_element_type=F32)


def _dot_tn(a, b):
    return lax.dot_general(a, b, (((0,), (0,)), ((), ())), preferred_element_type=F32)


def _norm_proj_kernel(x_ref, g_ref, w_ref, cos_ref, sin_ref, *out_refs, segs, tm):
    x = x_ref[...]
    h = _rms(x, g_ref[...]).astype(BF16)
    lane = lax.broadcasted_iota(I32, (tm, LANES), 1)
    first_half = (lane & (IDX_DIM // 2)) == 0
    for (c0, width, mode), o_ref in zip(segs, out_refs):
        y = _dot(h, w_ref[:, c0:c0 + width])
        if mode == "plain":
            o_ref[...] = y
            continue
        cos = cos_ref[...]
        sin = sin_ref[...]
        for c in range(0, width, LANES):
            yc = y[:, c:c + LANES]
            partner = jnp.where(first_half, pltpu.roll(yc, LANES - 32, 1), pltpu.roll(yc, 32, 1))
            r = yc * cos + partner * sin
            if mode == "tail":
                r = jnp.where(lane < IDX_DIM, r, yc * (IDX_HEADS ** -0.5))
            o_ref[:, c:c + LANES] = r


def _norm_proj(x, g, w, segs, cos=None, sin=None, tm=512):
    n, d = x.shape
    tm = min(tm, n)
    assert n % tm == 0
    if cos is None:
        cos = jnp.zeros((tm, LANES), F32)
        sin = cos
    p = cos.shape[0]
    assert p % tm == 0
    nper = p // tm
    kern = functools.partial(_norm_proj_kernel, segs=tuple(segs), tm=tm)
    return pl.pallas_call(
        kern,
        grid=(n // tm,),
        in_specs=[
            pl.BlockSpec((tm, d), lambda i: (i, 0)),
            pl.BlockSpec((1, d), lambda i: (0, 0)),
            pl.BlockSpec(w.shape, lambda i: (0, 0)),
            pl.BlockSpec((tm, LANES), lambda i: (i % nper, 0)),
            pl.BlockSpec((tm, LANES), lambda i: (i % nper, 0)),
        ],
        out_specs=[pl.BlockSpec((tm, wd), lambda i: (i, 0)) for (_, wd, _) in segs],
        out_shape=[jax.ShapeDtypeStruct((n, wd), F32) for (_, wd, _) in segs],
        compiler_params=_cparams(("parallel",)),
    )(x, g.reshape(1, d), w, cos, sin)


def _rope_tables(pos):
    half = IDX_DIM // 2
    inv = ROPE_THETA ** (-jnp.arange(half, dtype=F32) / half)
    ang = pos.astype(F32)[:, None] * inv[None, :]
    cos = jnp.cos(ang)
    sin = jnp.sin(ang)
    cos_t = jnp.tile(cos, (1, LANES // half))
    sin_t = jnp.tile(jnp.concatenate([-sin, sin], axis=1), (1, LANES // IDX_DIM))
    return cos_t, sin_t


def _out_proj_kernel(*refs, widths):
    n_in = len(widths)
    x_refs = refs[:n_in]
    w_ref, res_ref, o_ref = refs[n_in:]
    acc = res_ref[...]
    r0 = 0
    for x_ref, wd in zip(x_refs, widths):
        acc = acc + _dot(x_ref[...].astype(BF16), w_ref[r0:r0 + wd, :])
        r0 += wd
    o_ref[...] = acc


def _out_proj(xs, w, res, tm=512):
    n, d = res.shape
    tm = min(tm, n)
    widths = tuple(x.shape[1] for x in xs)
    kern = functools.partial(_out_proj_kernel, widths=widths)
    return pl.pallas_call(
        kern,
        grid=(n // tm,),
        in_specs=[pl.BlockSpec((tm, wd), lambda i: (i, 0)) for wd in widths]
        + [pl.BlockSpec(w.shape, lambda i: (0, 0)), pl.BlockSpec((tm, d), lambda i: (i, 0))],
        out_specs=pl.BlockSpec((tm, d), lambda i: (i, 0)),
        out_shape=jax.ShapeDtypeStruct((n, d), F32),
        compiler_params=_cparams(("parallel",)),
    )(*xs, w, res)


def _ffn_kernel(x_ref, g_ref, wu_ref, wd_ref, gf_ref, o_ref, h_scr, acc_scr, *, final_norm):
    f = pl.program_id(1)

    @pl.when(f == 0)
    def _():
        x = x_ref[...]
        h_scr[...] = _rms(x, g_ref[...]).astype(BF16)
        acc_scr[...] = x

    u = jnp.maximum(_dot(h_scr[...], wu_ref[...]), 0.0)
    acc_scr[...] += _dot((u * u).astype(BF16), wd_ref[...])

    @pl.when(f == pl.num_programs(1) - 1)
    def _():
        y = acc_scr[...]
        if final_norm:
            y = _rms(y, gf_ref[...])
        o_ref[...] = y


def _ffn(x, g, w_up, w_down, g_final=None, tm=512, tf=1024):
    n, d = x.shape
    dff = w_up.shape[1]
    tm = min(tm, n)
    final_norm = g_final is not None
    if g_final is None:
        g_final = g
    kern = functools.partial(_ffn_kernel, final_norm=final_norm)
    return pl.pallas_call(
        kern,
        grid=(n // tm, dff // tf),
        in_specs=[
            pl.BlockSpec((tm, d), lambda i, f: (i, 0)),
            pl.BlockSpec((1, d), lambda i, f: (0, 0)),
            pl.BlockSpec((d, tf), lambda i, f: (0, f)),
            pl.BlockSpec((tf, d), lambda i, f: (f, 0)),
            pl.BlockSpec((1, d), lambda i, f: (0, 0)),
        ],
        out_specs=pl.BlockSpec((tm, d), lambda i, f: (i, 0)),
        out_shape=jax.ShapeDtypeStruct((n, d), F32),
        scratch_shapes=[pltpu.VMEM((tm, d), BF16), pltpu.VMEM((tm, d), F32)],
        compiler_params=_cparams(("parallel", "arbitrary")),
    )(x, g.reshape(1, d), w_up, w_down, g_final.reshape(1, d))


def _lambda(lam_ref, lam_init):
    l = lam_ref[...]
    a = jnp.sum(l[0:1] * l[1:2], axis=1, keepdims=True)
    b = jnp.sum(l[2:3] * l[3:4], axis=1, keepdims=True)
    return jnp.exp(a) - jnp.exp(b) + lam_init


def _split_halves(q):
    lane = lax.broadcasted_iota(I32, q.shape, 1)
    lo = lane < DSA_HEAD_DIM
    return jnp.concatenate([jnp.where(lo, q, 0.0), jnp.where(lo, 0.0, q)], axis=0)


def _sortable_key(score):
    bits = pltpu.bitcast(score, I32)
    return jnp.where(bits >= 0, bits, bits ^ jnp.int32(0x7FFFFFFF))


def _kth_largest(count_ge, rows, kk):
    def body(it, t):
        cand = t + jnp.left_shift(jnp.int32(1), 31 - it)
        return jnp.where(count_ge(cand) >= kk, cand, t)

    return lax.fori_loop(0, 32, body, jnp.full((rows, 1), INT_MIN, I32))


def _diff_attn_kernel(lam_ref, g_ref, q_ref, k_ref, v_ref, o_ref, m_scr, l_scr, acc_scr, *, blk, lam_init):
    i = pl.program_id(2)
    q2 = _split_halves(q_ref[0] * (DIFF_QK_DIM ** -0.5)).astype(BF16)
    m_scr[...] = jnp.full(m_scr.shape, NEG_INF, F32)
    l_scr[...] = jnp.zeros(l_scr.shape, F32)
    acc_scr[...] = jnp.zeros(acc_scr.shape, F32)

    def step(j, on_diagonal):
        start = pl.multiple_of(j * blk, blk)
        kj = k_ref[0, pl.ds(start, blk), :].astype(BF16)
        vj = v_ref[0, pl.ds(start, blk), :].astype(BF16)
        s = _dot_nt(q2, kj)
        if on_diagonal:
            row = lax.broadcasted_iota(I32, (2 * blk, blk), 0)
            col = lax.broadcasted_iota(I32, (2 * blk, blk), 1)
            s = jnp.where(col <= jnp.where(row >= blk, row - blk, row), s, NEG_INF)
        m_old = m_scr[...]
        m_new = jnp.maximum(m_old, jnp.max(s, axis=1, keepdims=True))
        alpha = jnp.exp(m_old - m_new)
        p = jnp.exp(s - pltpu.repeat(m_new, blk // LANES, 1))
        l_scr[...] = alpha * l_scr[...] + jnp.sum(p, axis=1, keepdims=True)
        acc_scr[...] = alpha * acc_scr[...] + _dot(p.astype(BF16), vj)
        m_scr[...] = m_new

    def body(j, carry):
        step(j, False)
        return carry

    lax.fori_loop(0, i, body, 0)
    step(i, True)
    o = acc_scr[...] / l_scr[...]
    a = o[:blk] - _lambda(lam_ref, lam_init) * o[blk:]
    o_ref[0] = _rms(a, g_ref[...]) * (1.0 - lam_init)


def _diff_attn_prompt(q, k, v, lam_rows, g_head, lam_init, blk=256):
    b, t, _ = q.shape
    blk = min(blk, t)
    kern = functools.partial(_diff_attn_kernel, blk=blk, lam_init=lam_init)
    return pl.pallas_call(
        kern,
        grid=(b, DIFF_HEADS, t // blk),
        in_specs=[
            pl.BlockSpec(lam_rows.shape, lambda bi, h, i: (0, 0)),
            pl.BlockSpec((1, DIFF_V_DIM), lambda bi, h, i: (0, 0)),
            pl.BlockSpec((1, blk, LANES), lambda bi, h, i: (bi, i, h)),
            pl.BlockSpec((1, t, LANES), lambda bi, h, i: (bi, 0, h)),
            pl.BlockSpec((1, t, LANES), lambda bi, h, i: (bi, 0, h)),
        ],
        out_specs=pl.BlockSpec((1, blk, LANES), lambda bi, h, i: (bi, i, h)),
        out_shape=jax.ShapeDtypeStruct((b, t, HEAD_W), F32),
        scratch_shapes=[pltpu.VMEM((2 * blk, LANES), F32), pltpu.VMEM((2 * blk, LANES), F32),
                        pltpu.VMEM((2 * blk, LANES), F32)],
        compiler_params=_cparams(("parallel", "parallel", "parallel")),
    )(lam_rows, g_head.reshape(1, DIFF_V_DIM), q, k, v)


def _dsa_prompt_kernel(q_ref, iq_ref, tail_q_ref, k_ref, v_ref, tail_k_ref, o_ref, key_scr, bias_scr,
                       *, qb, t, topk):
    i = pl.program_id(1)
    qpos = i * qb + lax.broadcasted_iota(I32, (qb, t), 0)
    kpos = lax.broadcasted_iota(I32, (qb, t), 1)
    causal = kpos <= qpos

    ik = tail_k_ref[0, :, :IDX_DIM].astype(BF16)
    tq = tail_q_ref[0]
    score = jnp.zeros((qb, t), F32)
    for h in range(IDX_HEADS):
        qh = iq_ref[0, :, h * IDX_DIM:(h + 1) * IDX_DIM].astype(BF16)
        dots = _dot_nt(qh, ik) * (IDX_DIM ** -0.5)
        score = score + jnp.maximum(dots, 0.0) * tq[:, IDX_DIM + h:IDX_DIM + h + 1]
    score = score + 0.0
    key_scr[...] = jnp.where(causal, _sortable_key(score), INT_MIN)

    def count_ge(cand):
        return jnp.sum(jnp.where(key_scr[...] >= cand, 1.0, 0.0), axis=1, keepdims=True)

    thr = _kth_largest(count_ge, qb, float(topk))
    key = key_scr[...]
    above = key > thr
    n_above = jnp.sum(jnp.where(above, 1.0, 0.0), axis=1, keepdims=True)
    room = float(topk) - n_above
    tie = key == thr
    tri = (lax.broadcasted_iota(I32, (LANES, LANES), 0) <= lax.broadcasted_iota(I32, (LANES, LANES), 1))
    tri = jnp.where(tri, 1.0, 0.0).astype(BF16)
    before = jnp.zeros((qb, 1), F32)
    for c in range(0, t, LANES):
        tc = tie[:, c:c + LANES]
        rank = _dot(jnp.where(tc, 1.0, 0.0).astype(BF16), tri) + before
        sel = (above[:, c:c + LANES] | (tc & (rank <= room))) & causal[:, c:c + LANES]
        bias_scr[:, c:c + LANES] = jnp.where(sel, 0.0, NEG_INF)
        before = rank[:, LANES - 1:LANES]

    lane = lax.broadcasted_iota(I32, (qb, LANES), 1)
    for pb in range(DSA_HEADS // 2):
        cols = slice(pb * LANES, (pb + 1) * LANES)
        q2 = _split_halves(q_ref[0, :, cols] * (DSA_HEAD_DIM ** -0.5)).astype(BF16)
        kp = k_ref[0, :, cols].astype(BF16)
        vp = v_ref[0, :, cols].astype(BF16)
        outs = []
        for m in range(2):
            s = _dot_nt(q2[m * qb:(m + 1) * qb], kp) + bias_scr[...]
            mx = jnp.max(s, axis=1, keepdims=True)
            p = jnp.exp(s - mx)
            den = jnp.sum(p, axis=1, keepdims=True)
            outs.append(_dot(p.astype(BF16), vp) / den)
        o_ref[0, :, cols] = jnp.where(lane < DSA_HEAD_DIM, outs[0], outs[1])


def _dsa_prompt(q, k, v, iq, tail, qb=256):
    b, t, _ = q.shape
    qb = min(qb, t)
    topk = min(DSA_TOPK, t // 4)
    kern = functools.partial(_dsa_prompt_kernel, qb=qb, t=t, topk=topk)
    return pl.pallas_call(
        kern,
        grid=(b, t // qb),
        in_specs=[
            pl.BlockSpec((1, qb, HEAD_W), lambda bi, i: (bi, i, 0)),
            pl.BlockSpec((1, qb, IDX_HEADS * IDX_DIM), lambda bi, i: (bi, i, 0)),
            pl.BlockSpec((1, qb, LANES), lambda bi, i: (bi, i, 0)),
            pl.BlockSpec((1, t, HEAD_W), lambda bi, i: (bi, 0, 0)),
            pl.BlockSpec((1, t, HEAD_W), lambda bi, i: (bi, 0, 0)),
            pl.BlockSpec((1, t, LANES), lambda bi, i: (bi, 0, 0)),
        ],
        out_specs=pl.BlockSpec((1, qb, HEAD_W), lambda bi, i: (bi, i, 0)),
        out_shape=jax.ShapeDtypeStruct((b, t, HEAD_W), F32),
        scratch_shapes=[pltpu.VMEM((qb, t), I32), pltpu.VMEM((qb, t), F32)],
        compiler_params=_cparams(("parallel", "parallel")),
    )(q, iq, tail, k, v, tail)


def _dsa_sample_select_kernel(pt_ref, iq_ref, tail_ref, *refs, pg, tn, n_chunks, topk):
    page_refs = refs[:pg]
    bias_ref = refs[pg]
    key_scr = refs[pg + 1]
    j = pl.program_id(1)
    tail = tail_ref[0]
    iq = iq_ref[0]
    q_rows = jnp.concatenate([iq[:, h * IDX_DIM:(h + 1) * IDX_DIM] for h in range(IDX_HEADS)], axis=0).astype(BF16)
    w_rows = jnp.concatenate([tail[:, IDX_DIM + h:IDX_DIM + h + 1] for h in range(IDX_HEADS)], axis=0)

    def scores(dots):
        d = jnp.maximum(dots * (IDX_DIM ** -0.5), 0.0) * w_rows
        s = d[0:tn]
        for h in range(1, IDX_HEADS):
            s = s + d[h * tn:(h + 1) * tn]
        return s + 0.0

    k_step = jnp.concatenate([page_refs[g][0, 0] for g in range(pg)], axis=1).astype(BF16)
    key_step = _sortable_key(scores(_dot(q_rows, k_step)))
    for g in range(pg):
        key_scr[j * pg + g] = key_step[:, g * LANES:(g + 1) * LANES]

    @pl.when(j == pl.num_programs(1) - 1)
    def _():
        n_past = n_chunks - 1
        k_new = jnp.concatenate([tail[:, :IDX_DIM], jnp.zeros((LANES - tn, IDX_DIM), F32)], axis=0).astype(BF16)
        s_new = scores(_dot_nt(q_rows, k_new))
        row = lax.broadcasted_iota(I32, (tn, LANES), 0)
        col = lax.broadcasted_iota(I32, (tn, LANES), 1)
        key_scr[n_past] = jnp.where(col <= row, _sortable_key(s_new), INT_MIN)

        def count_ge(cand):
            n_part = min(4, n_past)
            per = n_past // n_part
            tot = jnp.where(key_scr[n_past] >= cand, 1.0, 0.0)
            for a in range(n_part):
                hi = n_past if a == n_part - 1 else (a + 1) * per
                tot = tot + jnp.sum(jnp.where(key_scr[a * per:hi] >= cand[None], 1.0, 0.0), axis=0)
            return jnp.sum(tot, axis=1, keepdims=True)

        thr = _kth_largest(count_ge, tn, float(topk))
        key = key_scr[...]
        above = key > thr[None]
        n_above = jnp.sum(jnp.sum(jnp.where(above, 1.0, 0.0), axis=0), axis=1, keepdims=True)
        room = float(topk) - n_above
        tie = jnp.where(key == thr[None], 1.0, 0.0)
        tri = (lax.broadcasted_iota(I32, (LANES, LANES), 0) <= lax.broadcasted_iota(I32, (LANES, LANES), 1))
        tri = jnp.where(tri, 1.0, 0.0).astype(BF16)
        within = _dot(tie.reshape(n_chunks * tn, LANES).astype(BF16), tri).reshape(n_chunks, tn, LANES)
        totals = within[:, :, LANES - 1:LANES]

        def chunk_body(c, before):
            rank = within[c] + before
            sel = above[c] | ((tie[c] > 0) & (rank <= room))
            bias_ref[0, c] = jnp.where(sel, 0.0, NEG_INF)
            return before + totals[c]

        before = jnp.zeros((tn, 1), F32)
        for c in range(n_chunks):
            before = chunk_body(c, before)
        bias_ref[0, n_past] = jnp.where(col <= row, bias_ref[0, n_past], NEG_INF)


def _dsa_sample_select(iq, tail, cache_ik_t, page_table, li, pg=32):
    bd, tn, _ = iq.shape
    n_pages = page_table.shape[1]
    page = cache_ik_t.shape[3]
    pg = min(pg, n_pages)
    assert page == LANES and n_pages % pg == 0
    n_chunks = n_pages + 1
    topk = min(DSA_TOPK, (n_pages * page + tn) // 4)
    kern = functools.partial(_dsa_sample_select_kernel, pg=pg, tn=tn, n_chunks=n_chunks, topk=topk)

    def page_spec(g):
        return pl.BlockSpec((1, 1, IDX_DIM, page), lambda b, j, pt: (li, pt[b, j * pg + g], 0, 0))

    return pl.pallas_call(
        kern,
        grid_spec=pltpu.PrefetchScalarGridSpec(
            num_scalar_prefetch=1,
            grid=(bd, n_pages // pg),
            in_specs=[
                pl.BlockSpec((1, tn, IDX_HEADS * IDX_DIM), lambda b, j, pt: (b, 0, 0)),
                pl.BlockSpec((1, tn, LANES), lambda b, j, pt: (b, 0, 0)),
            ] + [page_spec(g) for g in range(pg)],
            out_specs=pl.BlockSpec((1, n_chunks, tn, LANES), lambda b, j, pt: (b, 0, 0, 0)),
            scratch_shapes=[pltpu.VMEM((n_chunks, tn, LANES), I32)],
        ),
        out_shape=jax.ShapeDtypeStruct((bd, n_chunks, tn, LANES), F32),
        compiler_params=_cparams(("parallel", "arbitrary")),
    )(page_table, iq, tail, *([cache_ik_t] * pg))


def _paged_attn_kernel(pt_ref, lam_ref, g_ref, q_ref, kn_ref, vn_ref, bias_ref, *refs, pg, tn, diff, lam_init):
    k_refs = refs[:pg]
    v_refs = refs[pg:2 * pg]
    o_ref = refs[2 * pg]
    m_scr, l_scr, acc_scr = refs[2 * pg + 1:]
    j = pl.program_id(1)
    nblk = HEAD_W // LANES
    rows = 2 * tn
    scale = DIFF_QK_DIM ** -0.5

    def update(blk, s_parts, pv_fn):
        s = jnp.concatenate(s_parts, axis=1)
        m_old = m_scr[blk]
        m_new = jnp.maximum(m_old, jnp.max(s, axis=1, keepdims=True))
        alpha = jnp.exp(m_old - m_new)
        p = jnp.exp(s - pltpu.repeat(m_new, len(s_parts), 1))
        l_scr[blk] = alpha * l_scr[blk] + jnp.sum(p, axis=1, keepdims=True)
        pv = None
        for g in range(len(s_parts)):
            term = pv_fn(g, p[:, g * LANES:(g + 1) * LANES].astype(BF16))
            pv = term if pv is None else pv + term
        acc_scr[blk] = alpha * acc_scr[blk] + pv
        m_scr[blk] = m_new

    q_all = q_ref[0] * scale
    q2 = [_split_halves(q_all[:, b * LANES:(b + 1) * LANES]).astype(BF16) for b in range(nblk)]

    @pl.when(j == 0)
    def _():
        m_scr[...] = jnp.full(m_scr.shape, jnp.finfo(F32).min, F32)
        l_scr[...] = jnp.zeros(l_scr.shape, F32)
        acc_scr[...] = jnp.zeros(acc_scr.shape, F32)
        n_past = bias_ref.shape[1] - 1
        row = lax.broadcasted_iota(I32, (rows, LANES), 0)
        col = lax.broadcasted_iota(I32, (rows, LANES), 1)
        causal = col <= jnp.where(row >= tn, row - tn, row)
        bn = bias_ref[0, n_past]
        bn2 = jnp.concatenate([bn, bn], axis=0)
        pad = jnp.zeros((LANES - tn, LANES), F32)
        for blk in range(nblk):
            cols = slice(blk * LANES, (blk + 1) * LANES)
            kn = jnp.concatenate([kn_ref[0, :, cols], pad], axis=0).astype(BF16)
            vn = jnp.concatenate([vn_ref[0, :, cols], pad], axis=0).astype(BF16)
            s = jnp.where(causal, _dot_nt(q2[blk], kn), NEG_INF)
            if not diff:
                s = s + bn2
            update(blk, [s], lambda g, p: _dot(p, vn))

    if not diff:
        bias2 = []
        for g in range(pg):
            bp = bias_ref[0, j * pg + g]
            bias2.append(jnp.concatenate([bp, bp], axis=0))
    for blk in range(nblk):
        rows_blk = slice(blk * LANES, (blk + 1) * LANES)
        s_parts = [_dot(q2[blk], k_refs[g][0, 0, rows_blk, :].astype(BF16)) for g in range(pg)]
        if diff:
            update(blk, s_parts,
                   lambda g, p: _dot(p, v_refs[g][0, 0, pl.ds(blk, LANES, stride=nblk), :].astype(BF16)))
        else:
            update(blk, [s + b2 for s, b2 in zip(s_parts, bias2)],
                   lambda g, p: _dot_nt(p, v_refs[g][0, 0, rows_blk, :].astype(BF16)))

    @pl.when(j == pl.num_programs(1) - 1)
    def _():
        lane = lax.broadcasted_iota(I32, (tn, LANES), 1)
        for blk in range(nblk):
            o = acc_scr[blk] / l_scr[blk]
            if diff:
                a = o[:tn] - _lambda(lam_ref, lam_init) * o[tn:]
                r = _rms(a, g_ref[...]) * (1.0 - lam_init)
            else:
                r = jnp.where(lane < DSA_HEAD_DIM, o[:tn], o[tn:])
            o_ref[0, :, blk * LANES:(blk + 1) * LANES] = r


def _paged_attn(q, k_new, v_new, bias, cache_k, cache_v, page_table, li, lam_rows, g_head, lam_init, diff, pg=16):
    bd, tn, _ = q.shape
    n_pages = page_table.shape[1]
    pg = min(pg, n_pages)
    assert n_pages % pg == 0 and cache_k.shape[2:] == (HEAD_W, LANES) and cache_v.shape[2:] == (HEAD_W, LANES)
    n_chunks = n_pages + 1
    kern = functools.partial(_paged_attn_kernel, pg=pg, tn=tn, diff=diff, lam_init=lam_init)
    nblk = HEAD_W // LANES

    def page_spec(g):
        return pl.BlockSpec((1, 1, HEAD_W, LANES), lambda b, j, pt: (li, pt[b, j * pg + g], 0, 0))

    tok = pl.BlockSpec((1, tn, HEAD_W), lambda b, j, pt: (b, 0, 0))
    return pl.pallas_call(
        kern,
        grid_spec=pltpu.PrefetchScalarGridSpec(
            num_scalar_prefetch=1,
            grid=(bd, n_pages // pg),
            in_specs=[
                pl.BlockSpec(lam_rows.shape, lambda b, j, pt: (0, 0)),
                pl.BlockSpec((1, DIFF_V_DIM), lambda b, j, pt: (0, 0)),
                tok, tok, tok,
                pl.BlockSpec((1, n_chunks, tn, LANES), lambda b, j, pt: (b, 0, 0, 0)),
            ] + [page_spec(g) for g in range(pg)] + [page_spec(g) for g in range(pg)],
            out_specs=pl.BlockSpec((1, tn, HEAD_W), lambda b, j, pt: (b, 0, 0)),
            scratch_shapes=[pltpu.VMEM((nblk, 2 * tn, LANES), F32), pltpu.VMEM((nblk, 2 * tn, LANES), F32),
                            pltpu.VMEM((nblk, 2 * tn, LANES), F32)],
        ),
        out_shape=jax.ShapeDtypeStruct((bd, tn, HEAD_W), F32),
        compiler_params=_cparams(("parallel", "arbitrary")),
    )(page_table, lam_rows, g_head.reshape(1, DIFF_V_DIM), q, k_new, v_new, bias,
      *([cache_k] * pg), *([cache_v] * pg))


def _mlstm_kernel(q_ref, k_ref, v_ref, og_ref, gcol_ref, grow_ref, bcol_ref, brow_ref, gh_ref, c0_ref, n0_ref, m0_ref,
                  h_ref, c_out, n_out, m_out, c_scr, n_scr, m_scr, *, chunk, n_sub, dk, dv):
    j = pl.program_id(1)
    nh = MLSTM_HEADS

    @pl.when(j == 0)
    def _():
        c_scr[...] = c0_ref[0]
        for h in range(nh):
            n_scr[h] = n0_ref[0, h:h + 1, :]
            m_scr[h] = m0_ref[0, :, h:h + 1]

    r_i = lax.broadcasted_iota(I32, (chunk, chunk), 0)
    c_i = lax.broadcasted_iota(I32, (chunk, chunk), 1)
    causal = c_i <= r_i

    def gates(x, is_input_gate):
        ig = IGATE_SOFTCAP * jnp.tanh(x / IGATE_SOFTCAP)
        lf = jnp.minimum(x, 0.0) - jnp.log1p(jnp.exp(-jnp.abs(x)))
        return jnp.where(is_input_gate, ig, lf)

    def chunk_body(s, carry):
        r0 = pl.multiple_of(s * chunk, chunk)
        gc = gcol_ref[0, s] + bcol_ref[...]
        gr = grow_ref[0, s] + brow_ref[...]
        gc = gates(gc, lax.broadcasted_iota(I32, gc.shape, 1) < nh)
        gr = gates(gr, lax.broadcasted_iota(I32, gr.shape, 0) < nh)
        for h in range(nh):
            i_col, f_col = gc[:, h:h + 1], gc[:, nh + h:nh + h + 1]
            i_row, f_row = gr[h:h + 1, :], gr[nh + h:nh + h + 1, :]
            b_col = jnp.sum(jnp.where(causal, f_row, 0.0), axis=1, keepdims=True)
            b_row = jnp.sum(jnp.where(r_i <= c_i, f_col, 0.0), axis=0, keepdims=True)
            m_prev = m_scr[h]
            c_prev = c_scr[h]
            n_prev = n_scr[h]
            qc = q_ref[0, pl.ds(r0, chunk), h * dk:(h + 1) * dk] * (dk ** -0.5)
            kc = k_ref[0, pl.ds(r0, chunk), h * dk:(h + 1) * dk]
            vc = v_ref[0, pl.ds(r0, chunk), h * dv:(h + 1) * dv]
            dmat = jnp.where(causal, b_col - b_row + i_row, NEG_INF)
            inter = b_col + m_prev
            mt = jnp.maximum(inter, jnp.max(dmat, axis=1, keepdims=True))
            sm = _dot_nt(qc.astype(BF16), kc.astype(BF16)) * jnp.exp(dmat - mt)
            w_inter = jnp.exp(inter - mt)
            num = _dot(sm.astype(BF16), vc.astype(BF16)) + w_inter * _dot_nt(qc.astype(BF16), c_prev.astype(BF16))
            den = jnp.sum(sm, axis=1, keepdims=True) + w_inter * jnp.sum(qc * n_prev, axis=1, keepdims=True)
            hh = num / jnp.maximum(jnp.abs(den), jnp.exp(-mt))
            b_end = jnp.sum(f_row, axis=1, keepdims=True)
            d_end = b_end - b_col + i_col
            m_new = jnp.maximum(b_end + m_prev, jnp.max(d_end, axis=0, keepdims=True))
            w_end = jnp.exp(d_end - m_new)
            decay = jnp.exp(b_end + m_prev - m_new)
            kw = kc * w_end
            c_scr[h] = decay * c_prev + _dot_tn(vc.astype(BF16), kw.astype(BF16))
            n_scr[h] = decay * n_prev + jnp.sum(kw, axis=0, keepdims=True)
            m_scr[h] = m_new
            hn = _rms(hh, gh_ref[:, h * dv:(h + 1) * dv])
            og = og_ref[0, pl.ds(r0, chunk), h * dv:(h + 1) * dv]
            h_ref[0, pl.ds(r0, chunk), h * dv:(h + 1) * dv] = hn * (1.0 / (1.0 + jnp.exp(-og)))
        return carry

    lax.fori_loop(0, n_sub, chunk_body, 0)

    @pl.when(j == pl.num_programs(1) - 1)
    def _():
        c_out[0] = c_scr[...]
        for h in range(nh):
            n_out[0, h:h + 1, :] = n_scr[h]
            m_out[0, :, h:h + 1] = m_scr[h]


def _mlstm(q, k, v, og, gates_raw, b_i, b_f, g_head, c0, n0, m0, tb=512):
    b, t, _ = q.shape
    nh = MLSTM_HEADS
    dk = q.shape[2] // nh
    dv = v.shape[2] // nh
    chunk = math.gcd(t, MLSTM_CHUNK)
    tb = min(tb, t)
    n_sub = tb // chunk
    nc = t // chunk
    g = gates_raw[:, :, :2 * nh].reshape(b, nc, chunk, 2 * nh)
    g_row = jnp.swapaxes(g, 2, 3)
    bias = jnp.concatenate([b_i, b_f]).astype(F32)
    kern = functools.partial(_mlstm_kernel, chunk=chunk, n_sub=n_sub, dk=dk, dv=dv)
    tok = lambda w: pl.BlockSpec((1, tb, w), lambda bi, j: (bi, j, 0))
    outs = pl.pallas_call(
        kern,
        grid=(b, t // tb),
        in_specs=[
            tok(nh * dk), tok(nh * dk), tok(nh * dv), tok(nh * dv),
            pl.BlockSpec((1, n_sub, chunk, 2 * nh), lambda bi, j: (bi, j, 0, 0)),
            pl.BlockSpec((1, n_sub, 2 * nh, chunk), lambda bi, j: (bi, j, 0, 0)),
            pl.BlockSpec((1, 2 * nh), lambda bi, j: (0, 0)),
            pl.BlockSpec((2 * nh, 1), lambda bi, j: (0, 0)),
            pl.BlockSpec((1, nh * dv), lambda bi, j: (0, 0)),
            pl.BlockSpec((1, nh, dv, dk), lambda bi, j: (bi, 0, 0, 0)),
            pl.BlockSpec((1, nh, dk), lambda bi, j: (bi, 0, 0)),
            pl.BlockSpec((1, 1, nh), lambda bi, j: (bi, 0, 0)),
        ],
        out_specs=[
            tok(nh * dv),
            pl.BlockSpec((1, nh, dv, dk), lambda bi, j: (bi, 0, 0, 0)),
            pl.BlockSpec((1, nh, dk), lambda bi, j: (bi, 0, 0)),
            pl.BlockSpec((1, 1, nh), lambda bi, j: (bi, 0, 0)),
        ],
        out_shape=[
            jax.ShapeDtypeStruct((b, t, nh * dv), F32),
            jax.ShapeDtypeStruct((b, nh, dv, dk), F32),
            jax.ShapeDtypeStruct((b, nh, dk), F32),
            jax.ShapeDtypeStruct((b, 1, nh), F32),
        ],
        scratch_shapes=[pltpu.VMEM((nh, dv, dk), F32), pltpu.VMEM((nh, 1, dk), F32), pltpu.VMEM((nh, 1, 1), F32)],
        compiler_params=_cparams(("parallel", "arbitrary")),
    )(q, k, v, og, g, g_row, bias.reshape(1, 2 * nh), bias.reshape(2 * nh, 1), g_head.reshape(1, nh * dv),
      c0, n0, m0.reshape(b, 1, nh))
    hh, c, n, m = outs
    return hh, c, n, m.reshape(b, nh)


_EVEN_SEGS = tuple([(i * HEAD_W, HEAD_W, mode) for i, mode in enumerate(("rope", "rope", "plain", "rope", "rope", "plain"))]
                   + [(6 * HEAD_W, IDX_HEADS * IDX_DIM, "rope"), (6 * HEAD_W + IDX_HEADS * IDX_DIM, LANES, "tail")])


def _pad_cols(w, mult=LANES):
    pad = (-w.shape[1]) % mult
    return jnp.pad(w, ((0, 0), (0, pad))) if pad else w


def _attn_layer(yp, ys, w_in, w_out, lam_rows, g_head, g_mix, lam_init, caches, page_table, li, dims):
    b, t, bd, tn, past = dims
    d = yp.shape[1]
    w_in = _pad_cols(w_in.astype(BF16))
    w_out = w_out.astype(BF16)
    cache_dk, cache_dv, cache_sk, cache_sv, cache_ik = caches

    cos, sin = _rope_tables(jnp.arange(t))
    aq, ak, av, bq, bk, bv, iq, tail = _norm_proj(yp, g_mix, w_in, _EVEN_SEGS, cos, sin)
    r3 = lambda x: x.reshape(b, t, x.shape[-1])
    a_out = _diff_attn_prompt(r3(aq), r3(ak), r3(av), lam_rows, g_head, lam_init)
    b_out = _dsa_prompt(r3(bq), r3(bk), r3(bv), r3(iq), r3(tail))
    yp = _out_proj([a_out.reshape(b * t, HEAD_W), b_out.reshape(b * t, HEAD_W)], w_out, yp)
    new_p = (ak.reshape(b, t, DIFF_HEADS, 2, DIFF_QK_DIM), av.reshape(b, t, DIFF_HEADS, DIFF_V_DIM),
             bk.reshape(b, t, DSA_HEADS, DSA_HEAD_DIM), bv.reshape(b, t, DSA_HEADS, DSA_HEAD_DIM),
             tail.reshape(b, t, LANES)[:, :, :IDX_DIM])

    cos, sin = _rope_tables(jnp.tile(past + jnp.arange(tn), bd))
    aq, ak, av, bq, bk, bv, iq, tail = _norm_proj(ys, g_mix, w_in, _EVEN_SEGS, cos, sin)
    s3 = lambda x: x.reshape(bd, tn, x.shape[-1])
    npool, page = cache_dk.shape[1], cache_dk.shape[2]
    slot_last = lambda c: jnp.moveaxis(c, 2, -1).reshape(c.shape[0], npool, -1, page)
    bias = _dsa_sample_select(s3(iq), s3(tail), slot_last(cache_ik), page_table, li)
    dv_rows = cache_dv.reshape(cache_dv.shape[0], npool, page * DIFF_HEADS, DIFF_V_DIM)
    a_out = _paged_attn(s3(aq), s3(ak), s3(av), bias, slot_last(cache_dk), dv_rows, page_table, li,
                        lam_rows, g_head, lam_init, diff=True)
    b_out = _paged_attn(s3(bq), s3(bk), s3(bv), bias, slot_last(cache_sk), slot_last(cache_sv), page_table, li,
                        lam_rows, g_head, lam_init, diff=False)
    ys = _out_proj([a_out.reshape(bd * tn, HEAD_W), b_out.reshape(bd * tn, HEAD_W)], w_out, ys)
    new_s = (ak.reshape(bd, tn, DIFF_HEADS, 2, DIFF_QK_DIM), av.reshape(bd, tn, DIFF_HEADS, DIFF_V_DIM),
             bk.reshape(bd, tn, DSA_HEADS, DSA_HEAD_DIM), bv.reshape(bd, tn, DSA_HEADS, DSA_HEAD_DIM),
             tail.reshape(bd, tn, LANES)[:, :, :IDX_DIM])
    return yp, ys, new_p, new_s


def _mlstm_layer(y, bsz, t, w_in, w_out, b_i, b_f, g_head, g_mix, c0, n0, m0):
    d = y.shape[1]
    nh = MLSTM_HEADS
    dk = d // (2 * nh)
    dv = d // nh
    segs = ((0, nh * dk, "plain"), (nh * dk, nh * dk, "plain"), (2 * nh * dk, nh * dv, "plain"),
            (2 * nh * dk + nh * dv, nh * dv, "plain"), (2 * nh * dk + 2 * nh * dv, LANES, "plain"))
    q, k, v, og, gts = _norm_proj(y, g_mix, w_in, segs)
    r3 = lambda x: x.reshape(bsz, t, x.shape[-1])
    hh, c, n, m = _mlstm(r3(q), r3(k), r3(v), r3(og), r3(gts), b_i, b_f, g_head, c0, n0, m0)
    y = _out_proj([hh.reshape(bsz * t, nh * dv)], w_out, y)
    return y, (c, n, m)


def kernel(x_prompt, x_sample, cache_diff_k, cache_diff_v, cache_dsa_k, cache_dsa_v, cache_idx_k, state_mlstm_c, state_mlstm_n, state_mlstm_m, page_table, g_mix, g_ffn, g_final, w_attn_in, w_attn_out, lambda_q1, lambda_k1, lambda_q2, lambda_k2, g_diff_head, w_mlstm_in, b_mlstm_i, b_mlstm_f, g_mlstm_head, w_mlstm_out, w_ffn_up, w_ffn_down):
    b, t, d = x_prompt.shape
    bd, tn, _ = x_sample.shape
    depth = g_mix.shape[0]
    past = page_table.shape[1] * cache_idx_k.shape[2]
    nh = MLSTM_HEADS
    yp = x_prompt.reshape(b * t, d)
    ys = x_sample.reshape(bd * tn, d)
    new_p, new_s, st_p, st_s = [], [], [], []
    for layer in range(depth):
        li = layer // 2
        last = layer == depth - 1
        if layer % 2 == 0:
            lam_init = 0.8 - 0.6 * math.exp(-0.3 * layer)
            lam_rows = jnp.stack([lambda_q1[li], lambda_k1[li], lambda_q2[li], lambda_k2[li]]).astype(F32)
            yp, ys, np_, ns_ = _attn_layer(
                yp, ys, w_attn_in[li], w_attn_out[li], lam_rows, g_diff_head[li], g_mix[layer], lam_init,
                (cache_diff_k, cache_diff_v, cache_dsa_k, cache_dsa_v, cache_idx_k), page_table, li,
                (b, t, bd, tn, past))
            new_p.append(np_)
            new_s.append(ns_)
        else:
            w_in = _pad_cols(w_mlstm_in[li].astype(BF16))
            w_out = w_mlstm_out[li].astype(BF16)
            dk = d // (2 * nh)
            dv = d // nh
            zc = jnp.zeros((b, nh, dv, dk), F32)
            zn = jnp.zeros((b, nh, dk), F32)
            zm = jnp.zeros((b, nh), F32)
            yp, sp = _mlstm_layer(yp, b, t, w_in, w_out, b_mlstm_i[li], b_mlstm_f[li], g_mlstm_head[li],
                                  g_mix[layer], zc, zn, zm)
            ys, ss = _mlstm_layer(ys, bd, tn, w_in, w_out, b_mlstm_i[li], b_mlstm_f[li], g_mlstm_head[li],
                                  g_mix[layer], state_mlstm_c[li], state_mlstm_n[li], state_mlstm_m[li])
            st_p.append(sp)
            st_s.append(ss)
        w_up = w_ffn_up[layer].astype(BF16)
        w_down = w_ffn_down[layer].astype(BF16)
        gf = g_final if last else None
        yp = _ffn(yp, g_ffn[layer], w_up, w_down, gf)
        ys = _ffn(ys, g_ffn[layer], w_up, w_down, gf)
    stack = lambda items, idx: jnp.stack([it[idx] for it in items])
    return (yp.reshape(b, t, d), ys.reshape(bd, tn, d),
            stack(new_p, 0), stack(new_p, 1), stack(new_p, 2), stack(new_p, 3), stack(new_p, 4),
            stack(st_p, 0), stack(st_p, 1), stack(st_p, 2),
            stack(new_s, 0), stack(new_s, 1), stack(new_s, 2), stack(new_s, 3), stack(new_s, 4),
            stack(st_s, 0), stack(st_s, 1), stack(st_s, 2))
```

```python
import functools
import math

import jax
import jax.numpy as jnp
from jax import lax
from jax.experimental import pallas as pl
from jax.experimental.pallas import tpu as pltpu

F32 = jnp.float32
BF16 = jnp.bfloat16
I32 = jnp.int32

DIFF_HEADS = 4
DIFF_QK_DIM = 64
DIFF_V_DIM = 128
DSA_HEADS = 8
DSA_HEAD_DIM = 64
IDX_HEADS = 4
IDX_DIM = 64
DSA_TOPK = 256
MLSTM_HEADS = 4
MLSTM_CHUNK = 64
IGATE_SOFTCAP = 15.0
ROPE_THETA = 10000.0
RMS_EPS = 1e-6
HEAD_W = DIFF_HEADS * 2 * DIFF_QK_DIM

LANES = 128
SUBLANES = 8
VMEM_LIMIT = 56 * 1024 * 1024

INT_MIN = -(2 ** 31)
NEG_INF = float("-inf")


def _cparams(sem):
    return pltpu.CompilerParams(dimension_semantics=sem, vmem_limit_bytes=VMEM_LIMIT)


def _rms(x, g):
    return x * lax.rsqrt(jnp.mean(x * x, axis=-1, keepdims=True) + RMS_EPS) * g


def _dot(a, b):
    return jnp.dot(a, b, preferred_element_type=F32)


def _dot_nt(a, b):
    return lax.dot_general(a, b, (((1,), (1,)), ((), ())), preferred_element_type=F32)


def _dot_tn(a, b):
    return lax.dot_general(a, b, (((0,), (0,)), ((), ())), preferred_element_type=F32)


def _norm_proj_kernel(x_ref, g_ref, w_ref, cos_ref, sin_ref, *out_refs, segs, tm):
    x = x_ref[...]
    h = _rms(x, g_ref[...]).astype(BF16)
    lane = lax.broadcasted_iota(I32, (tm, LANES), 1)
    first_half = (lane & (IDX_DIM // 2)) == 0
    for (c0, width, mode), o_ref in zip(segs, out_refs):
        y = _dot(h, w_ref[:, c0:c0 + width])
        if mode == "plain":
            o_ref[...] = y
            continue
        cos = cos_ref[...]
        sin = sin_ref[...]
        for c in range(0, width, LANES):
            yc = y[:, c:c + LANES]
            partner = jnp.where(first_half, pltpu.roll(yc, LANES - 32, 1), pltpu.roll(yc, 32, 1))
            r = yc * cos + partner * sin
            if mode == "tail":
                r = jnp.where(lane < IDX_DIM, r, yc * (IDX_HEADS ** -0.5))
            o_ref[:, c:c + LANES] = r


def _norm_proj(x, g, w, segs, cos=None, sin=None, tm=512):
    n, d = x.shape
    tm = min(tm, n)
    assert n % tm == 0
    if cos is None:
        cos = jnp.zeros((tm, LANES), F32)
        sin = cos
    p = cos.shape[0]
    assert p % tm == 0
    nper = p // tm
    kern = functools.partial(_norm_proj_kernel, segs=tuple(segs), tm=tm)
    return pl.pallas_call(
        kern,
        grid=(n // tm,),
        in_specs=[
            pl.BlockSpec((tm, d), lambda i: (i, 0)),
            pl.BlockSpec((1, d), lambda i: (0, 0)),
            pl.BlockSpec(w.shape, lambda i: (0, 0)),
            pl.BlockSpec((tm, LANES), lambda i: (i % nper, 0)),
            pl.BlockSpec((tm, LANES), lambda i: (i % nper, 0)),
        ],
        out_specs=[pl.BlockSpec((tm, wd), lambda i: (i, 0)) for (_, wd, _) in segs],
        out_shape=[jax.ShapeDtypeStruct((n, wd), F32) for (_, wd, _) in segs],
        compiler_params=_cparams(("parallel",)),
    )(x, g.reshape(1, d), w, cos, sin)


def _rope_tables(pos):
    half = IDX_DIM // 2
    inv = ROPE_THETA ** (-jnp.arange(half, dtype=F32) / half)
    ang = pos.astype(F32)[:, None] * inv[None, :]
    cos = jnp.cos(ang)
    sin = jnp.sin(ang)
    cos_t = jnp.tile(cos, (1, LANES // half))
    sin_t = jnp.tile(jnp.concatenate([-sin, sin], axis=1), (1, LANES // IDX_DIM))
    return cos_t, sin_t


def _out_proj_kernel(*refs, widths):
    n_in = len(widths)
    x_refs = refs[:n_in]
    w_ref, res_ref, o_ref = refs[n_in:]
    acc = res_ref[...]
    r0 = 0
    for x_ref, wd in zip(x_refs, widths):
        acc = acc + _dot(x_ref[...].astype(BF16), w_ref[r0:r0 + wd, :])
        r0 += wd
    o_ref[...] = acc


def _out_proj(xs, w, res, tm=512):
    n, d = res.shape
    tm = min(tm, n)
    widths = tuple(x.shape[1] for x in xs)
    kern = functools.partial(_out_proj_kernel, widths=widths)
    return pl.pallas_call(
        kern,
        grid=(n // tm,),
        in_specs=[pl.BlockSpec((tm, wd), lambda i: (i, 0)) for wd in widths]
        + [pl.BlockSpec(w.shape, lambda i: (0, 0)), pl.BlockSpec((tm, d), lambda i: (i, 0))],
        out_specs=pl.BlockSpec((tm, d), lambda i: (i, 0)),
        out_shape=jax.ShapeDtypeStruct((n, d), F32),
        compiler_params=_cparams(("parallel",)),
    )(*xs, w, res)


def _ffn_kernel(x_ref, g_ref, wu_ref, wd_ref, gf_ref, o_ref, h_scr, acc_scr, *, final_norm):
    f = pl.program_id(1)

    @pl.when(f == 0)
    def _():
        x = x_ref[...]
        h_scr[...] = _rms(x, g_ref[...]).astype(BF16)
        acc_scr[...] = x

    u = jnp.maximum(_dot(h_scr[...], wu_ref[...]), 0.0)
    acc_scr[...] += _dot((u * u).astype(BF16), wd_ref[...])

    @pl.when(f == pl.num_programs(1) - 1)
    def _():
        y = acc_scr[...]
        if final_norm:
            y = _rms(y, gf_ref[...])
        o_ref[...] = y


def _ffn(x, g, w_up, w_down, g_final=None, tm=1024, tf=1024):
    n, d = x.shape
    dff = w_up.shape[1]
    tm = min(tm, n)
    final_norm = g_final is not None
    if g_final is None:
        g_final = g
    kern = functools.partial(_ffn_kernel, final_norm=final_norm)
    return pl.pallas_call(
        kern,
        grid=(n // tm, dff // tf),
        in_specs=[
            pl.BlockSpec((tm, d), lambda i, f: (i, 0)),
            pl.BlockSpec((1, d), lambda i, f: (0, 0)),
            pl.BlockSpec((d, tf), lambda i, f: (0, f)),
            pl.BlockSpec((tf, d), lambda i, f: (f, 0)),
            pl.BlockSpec((1, d), lambda i, f: (0, 0)),
        ],
        out_specs=pl.BlockSpec((tm, d), lambda i, f: (i, 0)),
        out_shape=jax.ShapeDtypeStruct((n, d), F32),
        scratch_shapes=[pltpu.VMEM((tm, d), BF16), pltpu.VMEM((tm, d), F32)],
        compiler_params=_cparams(("parallel", "arbitrary")),
    )(x, g.reshape(1, d), w_up, w_down, g_final.reshape(1, d))


def _lambda(lam_ref, lam_init):
    l = lam_ref[...]
    a = jnp.sum(l[0:1] * l[1:2], axis=1, keepdims=True)
    b = jnp.sum(l[2:3] * l[3:4], axis=1, keepdims=True)
    return jnp.exp(a) - jnp.exp(b) + lam_init


def _lane_tile(x, n):
    return x if n == 1 else jnp.concatenate([x] * n, axis=1)


def _split_halves(q):
    lane = lax.broadcasted_iota(I32, q.shape, 1)
    lo = lane < DSA_HEAD_DIM
    return jnp.concatenate([jnp.where(lo, q, 0.0), jnp.where(lo, 0.0, q)], axis=0)


def _key_to_float(key):
    return pltpu.bitcast(jnp.where(key >= 0, key, key ^ jnp.int32(0x7FFFFFFF)), F32)


def _kth_largest(count_ge, rows, kk):
    def body(it, t):
        cand = t + jnp.left_shift(jnp.int32(1), 31 - it)
        return jnp.where(count_ge(_key_to_float(cand)) >= kk, cand, t)

    return _key_to_float(lax.fori_loop(0, 32, body, jnp.full((rows, 1), INT_MIN, I32)))


def _diff_attn_kernel(lam_ref, g_ref, q_ref, k_ref, v_ref, o_ref, m_scr, l_scr, acc_scr, *, blk, lam_init):
    i = pl.program_id(2)
    q2 = _split_halves(q_ref[0] * (DIFF_QK_DIM ** -0.5)).astype(BF16)
    m_scr[...] = jnp.full(m_scr.shape, NEG_INF, F32)
    l_scr[...] = jnp.zeros(l_scr.shape, F32)
    acc_scr[...] = jnp.zeros(acc_scr.shape, F32)

    def step(j, on_diagonal):
        start = pl.multiple_of(j * blk, blk)
        kj = k_ref[0, pl.ds(start, blk), :].astype(BF16)
        vj = v_ref[0, pl.ds(start, blk), :].astype(BF16)
        s = _dot_nt(q2, kj)
        if on_diagonal:
            row = lax.broadcasted_iota(I32, (2 * blk, blk), 0)
            col = lax.broadcasted_iota(I32, (2 * blk, blk), 1)
            s = jnp.where(col <= jnp.where(row >= blk, row - blk, row), s, NEG_INF)
        m_old = m_scr[...]
        m_new = jnp.maximum(m_old, jnp.max(s, axis=1, keepdims=True))
        alpha = jnp.exp(m_old - m_new)
        p = jnp.exp(s - _lane_tile(m_new, blk // LANES))
        l_scr[...] = alpha * l_scr[...] + jnp.sum(p, axis=1, keepdims=True)
        acc_scr[...] = alpha * acc_scr[...] + _dot(p.astype(BF16), vj)
        m_scr[...] = m_new

    def body(j, carry):
        step(j, False)
        return carry

    lax.fori_loop(0, i, body, 0)
    step(i, True)
    o = acc_scr[...] / l_scr[...]
    a = o[:blk] - _lambda(lam_ref, lam_init) * o[blk:]
    o_ref[0] = _rms(a, g_ref[...]) * (1.0 - lam_init)


def _diff_attn_prompt(q, k, v, lam_rows, g_head, lam_init, blk=512):
    b, t, _ = q.shape
    blk = min(blk, t)
    kern = functools.partial(_diff_attn_kernel, blk=blk, lam_init=lam_init)
    return pl.pallas_call(
        kern,
        grid=(b, DIFF_HEADS, t // blk),
        in_specs=[
            pl.BlockSpec(lam_rows.shape, lambda bi, h, i: (0, 0)),
            pl.BlockSpec((1, DIFF_V_DIM), lambda bi, h, i: (0, 0)),
            pl.BlockSpec((1, blk, LANES), lambda bi, h, i: (bi, i, h)),
            pl.BlockSpec((1, t, LANES), lambda bi, h, i: (bi, 0, h)),
            pl.BlockSpec((1, t, LANES), lambda bi, h, i: (bi, 0, h)),
        ],
        out_specs=pl.BlockSpec((1, blk, LANES), lambda bi, h, i: (bi, i, h)),
        out_shape=jax.ShapeDtypeStruct((b, t, HEAD_W), F32),
        scratch_shapes=[pltpu.VMEM((2 * blk, LANES), F32), pltpu.VMEM((2 * blk, LANES), F32),
                        pltpu.VMEM((2 * blk, LANES), F32)],
        compiler_params=_cparams(("parallel", "parallel", "parallel")),
    )(lam_rows, g_head.reshape(1, DIFF_V_DIM), q, k, v)


def _dsa_prompt_kernel(q_ref, iq_ref, tail_q_ref, k_ref, v_ref, tail_k_ref, o_ref, key_scr, bias_scr,
                       *, qb, t, topk, wstep):
    i = pl.program_id(1)

    def body(w):
        qpos = i * qb + lax.broadcasted_iota(I32, (qb, w), 0)
        kpos = lax.broadcasted_iota(I32, (qb, w), 1)
        causal = kpos <= qpos

        ik = tail_k_ref[0, :w, :IDX_DIM].astype(BF16)
        tq = tail_q_ref[0]
        score = jnp.zeros((qb, w), F32)
        for h in range(IDX_HEADS):
            qh = iq_ref[0, :, h * IDX_DIM:(h + 1) * IDX_DIM].astype(BF16)
            dots = _dot_nt(qh, ik) * (IDX_DIM ** -0.5)
            score = score + jnp.maximum(dots, 0.0) * tq[:, IDX_DIM + h:IDX_DIM + h + 1]
        key_scr[:, :w] = jnp.where(causal, score, NEG_INF)

        def count_ge(cand):
            return jnp.sum(jnp.where(key_scr[:, :w] >= cand, 1.0, 0.0), axis=1, keepdims=True)

        thr = _kth_largest(count_ge, qb, float(topk))
        key = key_scr[:, :w]
        above = (key > thr) | (qpos < topk)
        n_above = jnp.sum(jnp.where(above, 1.0, 0.0), axis=1, keepdims=True)
        room = float(topk) - n_above
        tie = key == thr
        tri = (lax.broadcasted_iota(I32, (LANES, LANES), 0) <= lax.broadcasted_iota(I32, (LANES, LANES), 1))
        tri = jnp.where(tri, 1.0, 0.0).astype(BF16)
        before = jnp.zeros((qb, 1), F32)
        for c in range(0, w, LANES):
            tc = tie[:, c:c + LANES]
            rank = _dot(jnp.where(tc, 1.0, 0.0).astype(BF16), tri) + before
            sel = (above[:, c:c + LANES] | (tc & (rank <= room))) & causal[:, c:c + LANES]
            bias_scr[:, c:c + LANES] = jnp.where(sel, 0.0, NEG_INF)
            before = rank[:, LANES - 1:LANES]

        lane = lax.broadcasted_iota(I32, (qb, LANES), 1)
        for pb in range(DSA_HEADS // 2):
            cols = slice(pb * LANES, (pb + 1) * LANES)
            q2 = _split_halves(q_ref[0, :, cols] * (DSA_HEAD_DIM ** -0.5)).astype(BF16)
            kp = k_ref[0, :w, cols].astype(BF16)
            vp = v_ref[0, :w, cols].astype(BF16)
            outs = []
            for m in range(2):
                s = _dot_nt(q2[m * qb:(m + 1) * qb], kp) + bias_scr[:, :w]
                mx = jnp.max(s, axis=1, keepdims=True)
                p = jnp.exp(s - mx)
                den = jnp.sum(p, axis=1, keepdims=True)
                outs.append(_dot(p.astype(BF16), vp) / den)
            o_ref[0, :, cols] = jnp.where(lane < DSA_HEAD_DIM, outs[0], outs[1])

    n_w = t // wstep
    needed = ((i + 1) * qb + wstep - 1) // wstep
    for a in range(1, n_w + 1):
        pl.when(needed == a)(functools.partial(body, a * wstep))


def _dsa_prompt(q, k, v, iq, tail, qb=256, wstep=512):
    b, t, _ = q.shape
    qb = min(qb, t)
    wstep = min(wstep, t)
    assert t % wstep == 0 and wstep % qb == 0
    topk = min(DSA_TOPK, t // 4)
    kern = functools.partial(_dsa_prompt_kernel, qb=qb, t=t, topk=topk, wstep=wstep)
    return pl.pallas_call(
        kern,
        grid=(b, t // qb),
        in_specs=[
            pl.BlockSpec((1, qb, HEAD_W), lambda bi, i: (bi, i, 0)),
            pl.BlockSpec((1, qb, IDX_HEADS * IDX_DIM), lambda bi, i: (bi, i, 0)),
            pl.BlockSpec((1, qb, LANES), lambda bi, i: (bi, i, 0)),
            pl.BlockSpec((1, t, HEAD_W), lambda bi, i: (bi, 0, 0)),
            pl.BlockSpec((1, t, HEAD_W), lambda bi, i: (bi, 0, 0)),
            pl.BlockSpec((1, t, LANES), lambda bi, i: (bi, 0, 0)),
        ],
        out_specs=pl.BlockSpec((1, qb, HEAD_W), lambda bi, i: (bi, i, 0)),
        out_shape=jax.ShapeDtypeStruct((b, t, HEAD_W), F32),
        scratch_shapes=[pltpu.VMEM((qb, t), F32), pltpu.VMEM((qb, t), F32)],
        compiler_params=_cparams(("parallel", "parallel")),
    )(q, iq, tail, k, v, tail)


def _dsa_sample_select_kernel(pt_ref, iq_ref, tail_ref, *refs, pg, tn, n_chunks, topk):
    page_refs = refs[:pg]
    bias_ref = refs[pg]
    key_scr = refs[pg + 1]
    j = pl.program_id(1)
    tail = tail_ref[0]
    iq = iq_ref[0]
    q_rows = jnp.concatenate([iq[:, h * IDX_DIM:(h + 1) * IDX_DIM] for h in range(IDX_HEADS)], axis=0).astype(BF16)
    w_rows = jnp.concatenate([tail[:, IDX_DIM + h:IDX_DIM + h + 1] for h in range(IDX_HEADS)], axis=0)

    def scores(dots):
        d = jnp.maximum(dots * (IDX_DIM ** -0.5), 0.0) * w_rows
        s = d[0:tn]
        for h in range(1, IDX_HEADS):
            s = s + d[h * tn:(h + 1) * tn]
        return s

    k_step = jnp.concatenate([page_refs[g][0, 0] for g in range(pg)], axis=1).astype(BF16)
    s_step = scores(_dot(q_rows, k_step))
    for g in range(pg):
        key_scr[j * pg + g] = s_step[:, g * LANES:(g + 1) * LANES]

    @pl.when(j == pl.num_programs(1) - 1)
    def _():
        n_past = n_chunks - 1
        k_new = jnp.concatenate([tail[:, :IDX_DIM], jnp.zeros((LANES - tn, IDX_DIM), F32)], axis=0).astype(BF16)
        s_new = scores(_dot_nt(q_rows, k_new))
        row = lax.broadcasted_iota(I32, (tn, LANES), 0)
        col = lax.broadcasted_iota(I32, (tn, LANES), 1)
        key_scr[n_past] = jnp.where(col <= row, s_new, NEG_INF)

        def count_ge(cand):
            n_part = min(4, n_past)
            per = n_past // n_part
            tot = jnp.where(key_scr[n_past] >= cand, 1.0, 0.0)
            for a in range(n_part):
                hi = n_past if a == n_part - 1 else (a + 1) * per
                tot = tot + jnp.sum(jnp.where(key_scr[a * per:hi] >= cand[None], 1.0, 0.0), axis=0)
            return jnp.sum(tot, axis=1, keepdims=True)

        thr = _kth_largest(count_ge, tn, float(topk))
        key = key_scr[...]
        n_valid = n_past * LANES + 1 + lax.broadcasted_iota(I32, (tn, 1), 0)
        above = (key > thr[None]) | (n_valid <= topk)[None]
        n_above = jnp.sum(jnp.sum(jnp.where(above, 1.0, 0.0), axis=0), axis=1, keepdims=True)
        room = float(topk) - n_above
        tie = jnp.where(key == thr[None], 1.0, 0.0)
        tri = (lax.broadcasted_iota(I32, (LANES, LANES), 0) <= lax.broadcasted_iota(I32, (LANES, LANES), 1))
        tri = jnp.where(tri, 1.0, 0.0).astype(BF16)
        within = _dot(tie.reshape(n_chunks * tn, LANES).astype(BF16), tri).reshape(n_chunks, tn, LANES)
        totals = within[:, :, LANES - 1:LANES]

        def chunk_body(c, before):
            rank = within[c] + before
            sel = above[c] | ((tie[c] > 0) & (rank <= room))
            bias_ref[0, c] = jnp.where(sel, 0.0, NEG_INF)
            return before + totals[c]

        before = jnp.zeros((tn, 1), F32)
        for c in range(n_chunks):
            before = chunk_body(c, before)
        bias_ref[0, n_past] = jnp.where(col <= row, bias_ref[0, n_past], NEG_INF)


def _dsa_sample_select(iq, tail, cache_ik_t, page_table, li, pg=32):
    bd, tn, _ = iq.shape
    n_pages = page_table.shape[1]
    page = cache_ik_t.shape[3]
    pg = min(pg, n_pages)
    assert page == LANES and n_pages % pg == 0
    n_chunks = n_pages + 1
    topk = min(DSA_TOPK, (n_pages * page + tn) // 4)
    kern = functools.partial(_dsa_sample_select_kernel, pg=pg, tn=tn, n_chunks=n_chunks, topk=topk)

    def page_spec(g):
        return pl.BlockSpec((1, 1, IDX_DIM, page), lambda b, j, pt: (li, pt[b, j * pg + g], 0, 0))

    return pl.pallas_call(
        kern,
        grid_spec=pltpu.PrefetchScalarGridSpec(
            num_scalar_prefetch=1,
            grid=(bd, n_pages // pg),
            in_specs=[
                pl.BlockSpec((1, tn, IDX_HEADS * IDX_DIM), lambda b, j, pt: (b, 0, 0)),
                pl.BlockSpec((1, tn, LANES), lambda b, j, pt: (b, 0, 0)),
            ] + [page_spec(g) for g in range(pg)],
            out_specs=pl.BlockSpec((1, n_chunks, tn, LANES), lambda b, j, pt: (b, 0, 0, 0)),
            scratch_shapes=[pltpu.VMEM((n_chunks, tn, LANES), F32)],
        ),
        out_shape=jax.ShapeDtypeStruct((bd, n_chunks, tn, LANES), F32),
        compiler_params=_cparams(("parallel", "arbitrary")),
    )(page_table, iq, tail, *([cache_ik_t] * pg))


def _paged_attn_kernel(pt_ref, lam_ref, g_ref, q_ref, kn_ref, vn_ref, bias_ref, ck_ref, cv_ref, o_ref,
                       kbuf, vbuf, sem, m_scr, l_scr, acc_scr, *, pg, tn, diff, lam_init, li):
    b = pl.program_id(0)
    j = pl.program_id(1)
    n_j = pl.num_programs(1)
    nblk = HEAD_W // LANES
    rows = 2 * tn
    scale = DIFF_QK_DIM ** -0.5

    step = b * n_j + j
    slot = step % 2

    def page_copies(bb, jj, sl):
        cps = []
        for g in range(pg):
            pid = pt_ref[bb, jj * pg + g]
            cps.append(pltpu.make_async_copy(ck_ref.at[li, pid], kbuf.at[sl, g], sem.at[sl, 0]))
            cps.append(pltpu.make_async_copy(cv_ref.at[li, pid], vbuf.at[sl, g], sem.at[sl, 1]))
        return cps

    @pl.when(step == 0)
    def _():
        for cp in page_copies(b, j, slot):
            cp.start()

    @pl.when(step + 1 < pl.num_programs(0) * n_j)
    def _():
        wrap = j + 1 == n_j
        for cp in page_copies(jnp.where(wrap, b + 1, b), jnp.where(wrap, 0, j + 1), 1 - slot):
            cp.start()

    for cp in page_copies(b, j, slot):
        cp.wait()

    def update(blk, s_parts, pv_fn):
        s = jnp.concatenate(s_parts, axis=1)
        m_old = m_scr[blk]
        m_new = jnp.maximum(m_old, jnp.max(s, axis=1, keepdims=True))
        alpha = jnp.exp(m_old - m_new)
        p = jnp.exp(s - _lane_tile(m_new, len(s_parts)))
        l_scr[blk] = alpha * l_scr[blk] + jnp.sum(p, axis=1, keepdims=True)
        pv = None
        for g in range(len(s_parts)):
            term = pv_fn(g, p[:, g * LANES:(g + 1) * LANES].astype(BF16))
            pv = term if pv is None else pv + term
        acc_scr[blk] = alpha * acc_scr[blk] + pv
        m_scr[blk] = m_new

    q_all = q_ref[0] * scale
    q2 = [_split_halves(q_all[:, b * LANES:(b + 1) * LANES]).astype(BF16) for b in range(nblk)]

    @pl.when(j == 0)
    def _():
        m_scr[...] = jnp.full(m_scr.shape, jnp.finfo(F32).min, F32)
        l_scr[...] = jnp.zeros(l_scr.shape, F32)
        acc_scr[...] = jnp.zeros(acc_scr.shape, F32)
        n_past = bias_ref.shape[1] - 1
        row = lax.broadcasted_iota(I32, (rows, LANES), 0)
        col = lax.broadcasted_iota(I32, (rows, LANES), 1)
        causal = col <= jnp.where(row >= tn, row - tn, row)
        bn = bias_ref[0, n_past]
        bn2 = jnp.concatenate([bn, bn], axis=0)
        pad = jnp.zeros((LANES - tn, LANES), F32)
        for blk in range(nblk):
            cols = slice(blk * LANES, (blk + 1) * LANES)
            kn = jnp.concatenate([kn_ref[0, :, cols], pad], axis=0).astype(BF16)
            vn = jnp.concatenate([vn_ref[0, :, cols], pad], axis=0).astype(BF16)
            s = jnp.where(causal, _dot_nt(q2[blk], kn), NEG_INF)
            if not diff:
                s = s + bn2
            update(blk, [s], lambda g, p: _dot(p, vn))

    if not diff:
        bias2 = []
        for g in range(pg):
            bp = bias_ref[0, j * pg + g]
            bias2.append(jnp.concatenate([bp, bp], axis=0))
    for blk in range(nblk):
        rows_blk = slice(blk * LANES, (blk + 1) * LANES)
        s_parts = [_dot(q2[blk], kbuf[slot, g, rows_blk, :].astype(BF16)) for g in range(pg)]
        if diff:
            update(blk, s_parts,
                   lambda g, p: _dot(p, vbuf[slot, g, pl.ds(blk, LANES, stride=nblk), :].astype(BF16)))
        else:
            update(blk, [s + b2 for s, b2 in zip(s_parts, bias2)],
                   lambda g, p: _dot_nt(p, vbuf[slot, g, rows_blk, :].astype(BF16)))

    @pl.when(j == pl.num_programs(1) - 1)
    def _():
        lane = lax.broadcasted_iota(I32, (tn, LANES), 1)
        for blk in range(nblk):
            o = acc_scr[blk] / l_scr[blk]
            if diff:
                a = o[:tn] - _lambda(lam_ref, lam_init) * o[tn:]
                r = _rms(a, g_ref[...]) * (1.0 - lam_init)
            else:
                r = jnp.where(lane < DSA_HEAD_DIM, o[:tn], o[tn:])
            o_ref[0, :, blk * LANES:(blk + 1) * LANES] = r


def _paged_attn(q, k_new, v_new, bias, cache_k, cache_v, page_table, li, lam_rows, g_head, lam_init, diff, pg=16):
    bd, tn, _ = q.shape
    n_pages = page_table.shape[1]
    pg = min(pg, n_pages)
    assert n_pages % pg == 0 and cache_k.shape[2:] == (HEAD_W, LANES) and cache_v.shape[2:] == (HEAD_W, LANES)
    n_chunks = n_pages + 1
    kern = functools.partial(_paged_attn_kernel, pg=pg, tn=tn, diff=diff, lam_init=lam_init, li=li)
    nblk = HEAD_W // LANES
    tok = pl.BlockSpec((1, tn, HEAD_W), lambda b, j, pt: (b, 0, 0))
    stats = pltpu.VMEM((nblk, 2 * tn, LANES), F32)
    pages = pltpu.VMEM((2, pg, HEAD_W, LANES), F32)
    return pl.pallas_call(
        kern,
        grid_spec=pltpu.PrefetchScalarGridSpec(
            num_scalar_prefetch=1,
            grid=(bd, n_pages // pg),
            in_specs=[
                pl.BlockSpec(lam_rows.shape, lambda b, j, pt: (0, 0)),
                pl.BlockSpec((1, DIFF_V_DIM), lambda b, j, pt: (0, 0)),
                tok, tok, tok,
                pl.BlockSpec((1, n_chunks, tn, LANES), lambda b, j, pt: (b, 0, 0, 0)),
                pl.BlockSpec(memory_space=pl.ANY),
                pl.BlockSpec(memory_space=pl.ANY),
            ],
            out_specs=pl.BlockSpec((1, tn, HEAD_W), lambda b, j, pt: (b, 0, 0)),
            scratch_shapes=[pages, pages, pltpu.SemaphoreType.DMA((2, 2)), stats, stats, stats],
        ),
        out_shape=jax.ShapeDtypeStruct((bd, tn, HEAD_W), F32),
        compiler_params=_cparams(("arbitrary", "arbitrary")),
    )(page_table, lam_rows, g_head.reshape(1, DIFF_V_DIM), q, k_new, v_new, bias, cache_k, cache_v)


def _mlstm_kernel(q_ref, k_ref, v_ref, og_ref, gcol_ref, grow_ref, bcol_ref, brow_ref, gh_ref, c0_ref, n0_ref, m0_ref,
                  h_ref, c_out, n_out, m_out, c_scr, n_scr, m_scr, *, chunk, n_sub, dk, dv):
    j = pl.program_id(1)
    nh = MLSTM_HEADS

    @pl.when(j == 0)
    def _():
        c_scr[...] = c0_ref[0]
        for h in range(nh):
            n_scr[h] = n0_ref[0, h:h + 1, :]
            m_scr[h] = m0_ref[0, :, h:h + 1]

    r_i = lax.broadcasted_iota(I32, (chunk, chunk), 0)
    c_i = lax.broadcasted_iota(I32, (chunk, chunk), 1)
    causal = c_i <= r_i

    def gates(x, is_input_gate):
        ig = IGATE_SOFTCAP * jnp.tanh(x / IGATE_SOFTCAP)
        lf = jnp.minimum(x, 0.0) - jnp.log1p(jnp.exp(-jnp.abs(x)))
        return jnp.where(is_input_gate, ig, lf)

    def chunk_body(s, carry):
        r0 = pl.multiple_of(s * chunk, chunk)
        gc = gcol_ref[0, s] + bcol_ref[...]
        gr = grow_ref[0, s] + brow_ref[...]
        gc = gates(gc, lax.broadcasted_iota(I32, gc.shape, 1) < nh)
        gr = gates(gr, lax.broadcasted_iota(I32, gr.shape, 0) < nh)
        for h in range(nh):
            i_col, f_col = gc[:, h:h + 1], gc[:, nh + h:nh + h + 1]
            i_row, f_row = gr[h:h + 1, :], gr[nh + h:nh + h + 1, :]
            b_col = jnp.sum(jnp.where(causal, f_row, 0.0), axis=1, keepdims=True)
            b_row = jnp.sum(jnp.where(r_i <= c_i, f_col, 0.0), axis=0, keepdims=True)
            m_prev = m_scr[h]
            c_prev = c_scr[h]
            n_prev = n_scr[h]
            qc = q_ref[0, pl.ds(r0, chunk), h * dk:(h + 1) * dk] * (dk ** -0.5)
            kc = k_ref[0, pl.ds(r0, chunk), h * dk:(h + 1) * dk]
            vc = v_ref[0, pl.ds(r0, chunk), h * dv:(h + 1) * dv]
            dmat = jnp.where(causal, b_col - b_row + i_row, NEG_INF)
            inter = b_col + m_prev
            mt = jnp.maximum(inter, jnp.max(dmat, axis=1, keepdims=True))
            sm = _dot_nt(qc.astype(BF16), kc.astype(BF16)) * jnp.exp(dmat - mt)
            w_inter = jnp.exp(inter - mt)
            num = _dot(sm.astype(BF16), vc.astype(BF16)) + w_inter * _dot_nt(qc.astype(BF16), c_prev.astype(BF16))
            den = jnp.sum(sm, axis=1, keepdims=True) + w_inter * jnp.sum(qc * n_prev, axis=1, keepdims=True)
            hh = num / jnp.maximum(jnp.abs(den), jnp.exp(-mt))
            b_end = jnp.sum(f_row, axis=1, keepdims=True)
            d_end = b_end - b_col + i_col
            m_new = jnp.maximum(b_end + m_prev, jnp.max(d_end, axis=0, keepdims=True))
            w_end = jnp.exp(d_end - m_new)
            decay = jnp.exp(b_end + m_prev - m_new)
            kw = kc * w_end
            c_scr[h] = decay * c_prev + _dot_tn(vc.astype(BF16), kw.astype(BF16))
            n_scr[h] = decay * n_prev + jnp.sum(kw, axis=0, keepdims=True)
            m_scr[h] = m_new
            hn = _rms(hh, gh_ref[:, h * dv:(h + 1) * dv])
            og = og_ref[0, pl.ds(r0, chunk), h * dv:(h + 1) * dv]
            h_ref[0, pl.ds(r0, chunk), h * dv:(h + 1) * dv] = hn * (1.0 / (1.0 + jnp.exp(-og)))
        return carry

    lax.fori_loop(0, n_sub, chunk_body, 0)

    @pl.when(j == pl.num_programs(1) - 1)
    def _():
        c_out[0] = c_scr[...]
        for h in range(nh):
            n_out[0, h:h + 1, :] = n_scr[h]
            m_out[0, :, h:h + 1] = m_scr[h]


def _mlstm(q, k, v, og, gates_raw, b_i, b_f, g_head, c0, n0, m0, tb=512):
    b, t, _ = q.shape
    nh = MLSTM_HEADS
    dk = q.shape[2] // nh
    dv = v.shape[2] // nh
    chunk = math.gcd(t, MLSTM_CHUNK)
    tb = min(tb, t)
    n_sub = tb // chunk
    nc = t // chunk
    g = gates_raw[:, :, :2 * nh].reshape(b, nc, chunk, 2 * nh)
    g_row = jnp.swapaxes(g, 2, 3)
    bias = jnp.concatenate([b_i, b_f]).astype(F32)
    kern = functools.partial(_mlstm_kernel, chunk=chunk, n_sub=n_sub, dk=dk, dv=dv)
    tok = lambda w: pl.BlockSpec((1, tb, w), lambda bi, j: (bi, j, 0))
    outs = pl.pallas_call(
        kern,
        grid=(b, t // tb),
        in_specs=[
            tok(nh * dk), tok(nh * dk), tok(nh * dv), tok(nh * dv),
            pl.BlockSpec((1, n_sub, chunk, 2 * nh), lambda bi, j: (bi, j, 0, 0)),
            pl.BlockSpec((1, n_sub, 2 * nh, chunk), lambda bi, j: (bi, j, 0, 0)),
            pl.BlockSpec((1, 2 * nh), lambda bi, j: (0, 0)),
            pl.BlockSpec((2 * nh, 1), lambda bi, j: (0, 0)),
            pl.BlockSpec((1, nh * dv), lambda bi, j: (0, 0)),
            pl.BlockSpec((1, nh, dv, dk), lambda bi, j: (bi, 0, 0, 0)),
            pl.BlockSpec((1, nh, dk), lambda bi, j: (bi, 0, 0)),
            pl.BlockSpec((1, 1, nh), lambda bi, j: (bi, 0, 0)),
        ],
        out_specs=[
            tok(nh * dv),
            pl.BlockSpec((1, nh, dv, dk), lambda bi, j: (bi, 0, 0, 0)),
            pl.BlockSpec((1, nh, dk), lambda bi, j: (bi, 0, 0)),
            pl.BlockSpec((1, 1, nh), lambda bi, j: (bi, 0, 0)),
        ],
        out_shape=[
            jax.ShapeDtypeStruct((b, t, nh * dv), F32),
            jax.ShapeDtypeStruct((b, nh, dv, dk), F32),
            jax.ShapeDtypeStruct((b, nh, dk), F32),
            jax.ShapeDtypeStruct((b, 1, nh), F32),
        ],
        scratch_shapes=[pltpu.VMEM((nh, dv, dk), F32), pltpu.VMEM((nh, 1, dk), F32), pltpu.VMEM((nh, 1, 1), F32)],
        compiler_params=_cparams(("parallel", "arbitrary")),
    )(q, k, v, og, g, g_row, bias.reshape(1, 2 * nh), bias.reshape(2 * nh, 1), g_head.reshape(1, nh * dv),
      c0, n0, m0.reshape(b, 1, nh))
    hh, c, n, m = outs
    return hh, c, n, m.reshape(b, nh)


_EVEN_SEGS = tuple([(i * HEAD_W, HEAD_W, mode) for i, mode in enumerate(("rope", "rope", "plain", "rope", "rope", "plain"))]
                   + [(6 * HEAD_W, IDX_HEADS * IDX_DIM, "rope"), (6 * HEAD_W + IDX_HEADS * IDX_DIM, LANES, "tail")])


def _pad_cols(w, mult=LANES):
    pad = (-w.shape[1]) % mult
    return jnp.pad(w, ((0, 0), (0, pad))) if pad else w


def _attn_layer(yp, ys, w_in, w_out, lam_rows, g_head, g_mix, lam_init, caches, page_table, li, dims):
    b, t, bd, tn, past = dims
    d = yp.shape[1]
    w_in = _pad_cols(w_in.astype(BF16))
    w_out = w_out.astype(BF16)
    cache_dk, cache_dv, cache_sk, cache_sv, cache_ik = caches

    cos, sin = _rope_tables(jnp.arange(t))
    aq, ak, av, bq, bk, bv, iq, tail = _norm_proj(yp, g_mix, w_in, _EVEN_SEGS, cos, sin)
    r3 = lambda x: x.reshape(b, t, x.shape[-1])
    a_out = _diff_attn_prompt(r3(aq), r3(ak), r3(av), lam_rows, g_head, lam_init)
    b_out = _dsa_prompt(r3(bq), r3(bk), r3(bv), r3(iq), r3(tail))
    yp = _out_proj([a_out.reshape(b * t, HEAD_W), b_out.reshape(b * t, HEAD_W)], w_out, yp)
    new_p = (ak.reshape(b, t, DIFF_HEADS, 2, DIFF_QK_DIM), av.reshape(b, t, DIFF_HEADS, DIFF_V_DIM),
             bk.reshape(b, t, DSA_HEADS, DSA_HEAD_DIM), bv.reshape(b, t, DSA_HEADS, DSA_HEAD_DIM),
             tail.reshape(b, t, LANES)[:, :, :IDX_DIM])

    cos, sin = _rope_tables(jnp.tile(past + jnp.arange(tn), bd))
    aq, ak, av, bq, bk, bv, iq, tail = _norm_proj(ys, g_mix, w_in, _EVEN_SEGS, cos, sin)
    s3 = lambda x: x.reshape(bd, tn, x.shape[-1])
    npool, page = cache_dk.shape[1], cache_dk.shape[2]
    slot_last = lambda c: jnp.moveaxis(c, 2, -1).reshape(c.shape[0], npool, -1, page)
    bias = _dsa_sample_select(s3(iq), s3(tail), slot_last(cache_ik), page_table, li)
    dv_rows = cache_dv.reshape(cache_dv.shape[0], npool, page * DIFF_HEADS, DIFF_V_DIM)
    a_out = _paged_attn(s3(aq), s3(ak), s3(av), bias, slot_last(cache_dk), dv_rows, page_table, li,
                        lam_rows, g_head, lam_init, diff=True)
    b_out = _paged_attn(s3(bq), s3(bk), s3(bv), bias, slot_last(cache_sk), slot_last(cache_sv), page_table, li,
                        lam_rows, g_head, lam_init, diff=False)
    ys = _out_proj([a_out.reshape(bd * tn, HEAD_W), b_out.reshape(bd * tn, HEAD_W)], w_out, ys)
    new_s = (ak.reshape(bd, tn, DIFF_HEADS, 2, DIFF_QK_DIM), av.reshape(bd, tn, DIFF_HEADS, DIFF_V_DIM),
             bk.reshape(bd, tn, DSA_HEADS, DSA_HEAD_DIM), bv.reshape(bd, tn, DSA_HEADS, DSA_HEAD_DIM),
             tail.reshape(bd, tn, LANES)[:, :, :IDX_DIM])
    return yp, ys, new_p, new_s


def _mlstm_layer(y, bsz, t, w_in, w_out, b_i, b_f, g_head, g_mix, c0, n0, m0):
    d = y.shape[1]
    nh = MLSTM_HEADS
    dk = d // (2 * nh)
    dv = d // nh
    segs = ((0, nh * dk, "plain"), (nh * dk, nh * dk, "plain"), (2 * nh * dk, nh * dv, "plain"),
            (2 * nh * dk + nh * dv, nh * dv, "plain"), (2 * nh * dk + 2 * nh * dv, LANES, "plain"))
    q, k, v, og, gts = _norm_proj(y, g_mix, w_in, segs)
    r3 = lambda x: x.reshape(bsz, t, x.shape[-1])
    hh, c, n, m = _mlstm(r3(q), r3(k), r3(v), r3(og), r3(gts), b_i, b_f, g_head, c0, n0, m0)
    y = _out_proj([hh.reshape(bsz * t, nh * dv)], w_out, y)
    return y, (c, n, m)


def kernel(x_prompt, x_sample, cache_diff_k, cache_diff_v, cache_dsa_k, cache_dsa_v, cache_idx_k, state_mlstm_c, state_mlstm_n, state_mlstm_m, page_table, g_mix, g_ffn, g_final, w_attn_in, w_attn_out, lambda_q1, lambda_k1, lambda_q2, lambda_k2, g_diff_head, w_mlstm_in, b_mlstm_i, b_mlstm_f, g_mlstm_head, w_mlstm_out, w_ffn_up, w_ffn_down):
    b, t, d = x_prompt.shape
    bd, tn, _ = x_sample.shape
    depth = g_mix.shape[0]
    past = page_table.shape[1] * cache_idx_k.shape[2]
    nh = MLSTM_HEADS
    yp = x_prompt.reshape(b * t, d)
    ys = x_sample.reshape(bd * tn, d)
    new_p, new_s, st_p, st_s = [], [], [], []
    for layer in range(depth):
        li = layer // 2
        last = layer == depth - 1
        if layer % 2 == 0:
            lam_init = 0.8 - 0.6 * math.exp(-0.3 * layer)
            lam_rows = jnp.stack([lambda_q1[li], lambda_k1[li], lambda_q2[li], lambda_k2[li]]).astype(F32)
            yp, ys, np_, ns_ = _attn_layer(
                yp, ys, w_attn_in[li], w_attn_out[li], lam_rows, g_diff_head[li], g_mix[layer], lam_init,
                (cache_diff_k, cache_diff_v, cache_dsa_k, cache_dsa_v, cache_idx_k), page_table, li,
                (b, t, bd, tn, past))
            new_p.append(np_)
            new_s.append(ns_)
        else:
            w_in = _pad_cols(w_mlstm_in[li].astype(BF16))
            w_out = w_mlstm_out[li].astype(BF16)
            dk = d // (2 * nh)
            dv = d // nh
            zc = jnp.zeros((b, nh, dv, dk), F32)
            zn = jnp.zeros((b, nh, dk), F32)
            zm = jnp.zeros((b, nh), F32)
            yp, sp = _mlstm_layer(yp, b, t, w_in, w_out, b_mlstm_i[li], b_mlstm_f[li], g_mlstm_head[li],
                                  g_mix[layer], zc, zn, zm)
            ys, ss = _mlstm_layer(ys, bd, tn, w_in, w_out, b_mlstm_i[li], b_mlstm_f[li], g_mlstm_head[li],
                                  g_mix[layer], state_mlstm_c[li], state_mlstm_n[li], state_mlstm_m[li])
            st_p.append(sp)
            st_s.append(ss)
        w_up = w_ffn_up[layer].astype(BF16)
        w_down = w_ffn_down[layer].astype(BF16)
        gf = g_final if last else None
        yp = _ffn(yp, g_ffn[layer], w_up, w_down, gf)
        ys = _ffn(ys, g_ffn[layer], w_up, w_down, gf)
    stack = lambda items, idx: jnp.stack([it[idx] for it in items])
    return (yp.reshape(b, t, d), ys.reshape(bd, tn, d),
            stack(new_p, 0), stack(new_p, 1), stack(new_p, 2), stack(new_p, 3), stack(new_p, 4),
            stack(st_p, 0), stack(st_p, 1), stack(st_p, 2),
            stack(new_s, 0), stack(new_s, 1), stack(new_s, 2), stack(new_s, 3), stack(new_s, 4),
            stack(st_s, 0), stack(st_s, 1), stack(st_s, 2))
```

```python
import functools
import math

import jax
import jax.numpy as jnp
from jax import lax
from jax.experimental import pallas as pl
from jax.experimental.pallas import tpu as pltpu

F32 = jnp.float32
BF16 = jnp.bfloat16
I32 = jnp.int32

DIFF_HEADS = 4
DIFF_QK_DIM = 64
DIFF_V_DIM = 128
DSA_HEADS = 8
DSA_HEAD_DIM = 64
IDX_HEADS = 4
IDX_DIM = 64
DSA_TOPK = 256
MLSTM_HEADS = 4
MLSTM_CHUNK = 64
IGATE_SOFTCAP = 15.0
ROPE_THETA = 10000.0
RMS_EPS = 1e-6
HEAD_W = DIFF_HEADS * 2 * DIFF_QK_DIM

LANES = 128
SUBLANES = 8
VMEM_LIMIT = 56 * 1024 * 1024

INT_MIN = -(2 ** 31)
NEG_INF = float("-inf")


def _cparams(sem):
    return pltpu.CompilerParams(dimension_semantics=sem, vmem_limit_bytes=VMEM_LIMIT)


def _rms(x, g):
    return x * lax.rsqrt(jnp.mean(x * x, axis=-1, keepdims=True) + RMS_EPS) * g


def _dot(a, b):
    return jnp.dot(a, b, preferred_element_type=F32)


def _dot_nt(a, b):
    return lax.dot_general(a, b, (((1,), (1,)), ((), ())), preferred_element_type=F32)


def _dot_tn(a, b):
    return lax.dot_general(a, b, (((0,), (0,)), ((), ())), preferred_element_type=F32)


def _norm_proj_kernel(x_ref, g_ref, w_ref, cos_ref, sin_ref, *out_refs, segs, t_segs, tm):
    x = x_ref[...]
    h = _rms(x, g_ref[...]).astype(BF16)
    lane = lax.broadcasted_iota(I32, (tm, LANES), 1)
    first_half = (lane & (IDX_DIM // 2)) == 0
    t_refs = dict(zip(t_segs, out_refs[len(segs):]))
    for si, ((c0, width, mode), o_ref) in enumerate(zip(segs, out_refs)):
        y = _dot(h, w_ref[:, c0:c0 + width])
        for c in range(0, width, LANES):
            r = y[:, c:c + LANES]
            if mode != "plain":
                partner = jnp.where(first_half, pltpu.roll(r, LANES - 32, 1), pltpu.roll(r, 32, 1))
                rot = r * cos_ref[...] + partner * sin_ref[...]
                if mode == "tail":
                    rot = jnp.where(lane < IDX_DIM, rot, r * (IDX_HEADS ** -0.5))
                r = rot
            o_ref[:, c:c + LANES] = r
            if si in t_refs:
                rt = r.T
                if mode == "tail":
                    t_refs[si][0] = rt[:IDX_DIM]
                else:
                    t_refs[si][0, c:c + LANES, :] = rt


def _norm_proj(x, g, w, segs, cos=None, sin=None, t_segs=(), tm=512):
    n, d = x.shape
    tm = min(tm, n)
    assert n % tm == 0
    if cos is None:
        cos = jnp.zeros((tm, LANES), F32)
        sin = cos
    p = cos.shape[0]
    assert p % tm == 0 and n % p == 0
    nper = p // tm
    kern = functools.partial(_norm_proj_kernel, segs=tuple(segs), t_segs=tuple(t_segs), tm=tm)
    t_width = lambda si: IDX_DIM if segs[si][2] == "tail" else segs[si][1]
    return pl.pallas_call(
        kern,
        grid=(n // tm,),
        in_specs=[
            pl.BlockSpec((tm, d), lambda i: (i, 0)),
            pl.BlockSpec((1, d), lambda i: (0, 0)),
            pl.BlockSpec(w.shape, lambda i: (0, 0)),
            pl.BlockSpec((tm, LANES), lambda i: (i % nper, 0)),
            pl.BlockSpec((tm, LANES), lambda i: (i % nper, 0)),
        ],
        out_specs=[pl.BlockSpec((tm, wd), lambda i: (i, 0)) for (_, wd, _) in segs]
        + [pl.BlockSpec((1, t_width(si), tm), lambda i: (i // nper, 0, i % nper)) for si in t_segs],
        out_shape=[jax.ShapeDtypeStruct((n, wd), F32) for (_, wd, _) in segs]
        + [jax.ShapeDtypeStruct((n // p, t_width(si), p), F32) for si in t_segs],
        compiler_params=_cparams(("parallel",)),
    )(x, g.reshape(1, d), w, cos, sin)


def _rope_tables(pos):
    half = IDX_DIM // 2
    inv = ROPE_THETA ** (-jnp.arange(half, dtype=F32) / half)
    ang = pos.astype(F32)[:, None] * inv[None, :]
    cos = jnp.cos(ang)
    sin = jnp.sin(ang)
    cos_t = jnp.tile(cos, (1, LANES // half))
    sin_t = jnp.tile(jnp.concatenate([-sin, sin], axis=1), (1, LANES // IDX_DIM))
    return cos_t, sin_t


def _mix_ffn_kernel(*refs, widths, final_norm):
    n_in = len(widths)
    x_refs = refs[:n_in]
    wo_ref, res_ref, g_ref, wu_ref, wd_ref, gf_ref, o_ref, h_scr, acc_scr = refs[n_in:]
    f = pl.program_id(1)

    @pl.when(f == 0)
    def _():
        y = res_ref[...]
        r0 = 0
        for x_ref, wd in zip(x_refs, widths):
            y = y + _dot(x_ref[...].astype(BF16), wo_ref[r0:r0 + wd, :])
            r0 += wd
        h_scr[...] = _rms(y, g_ref[...]).astype(BF16)
        acc_scr[...] = y

    u = jnp.maximum(_dot(h_scr[...], wu_ref[...]), 0.0)
    acc_scr[...] += _dot((u * u).astype(BF16), wd_ref[...])

    @pl.when(f == pl.num_programs(1) - 1)
    def _():
        y = acc_scr[...]
        if final_norm:
            y = _rms(y, gf_ref[...])
        o_ref[...] = y


def _mix_ffn(xs, w_out, res, g, w_up, w_down, g_final=None, tm=512, tf=1024):
    n, d = res.shape
    dff = w_up.shape[1]
    tm = min(tm, n)
    widths = tuple(x.shape[1] for x in xs)
    final_norm = g_final is not None
    if g_final is None:
        g_final = g
    kern = functools.partial(_mix_ffn_kernel, widths=widths, final_norm=final_norm)
    row = lambda wd: pl.BlockSpec((tm, wd), lambda i, f: (i, 0))
    vec = pl.BlockSpec((1, d), lambda i, f: (0, 0))
    return pl.pallas_call(
        kern,
        grid=(n // tm, dff // tf),
        in_specs=[row(wd) for wd in widths] + [
            pl.BlockSpec(w_out.shape, lambda i, f: (0, 0)),
            row(d),
            vec,
            pl.BlockSpec((d, tf), lambda i, f: (0, f)),
            pl.BlockSpec((tf, d), lambda i, f: (f, 0)),
            vec,
        ],
        out_specs=row(d),
        out_shape=jax.ShapeDtypeStruct((n, d), F32),
        scratch_shapes=[pltpu.VMEM((tm, d), BF16), pltpu.VMEM((tm, d), F32)],
        compiler_params=_cparams(("parallel", "arbitrary")),
    )(*xs, w_out, res, g.reshape(1, d), w_up, w_down, g_final.reshape(1, d))


def _lambda(lam_ref, lam_init):
    l = lam_ref[...]
    a = jnp.sum(l[0:1] * l[1:2], axis=1, keepdims=True)
    b = jnp.sum(l[2:3] * l[3:4], axis=1, keepdims=True)
    return jnp.exp(a) - jnp.exp(b) + lam_init


def _lane_tile(x, n):
    return x if n == 1 else jnp.concatenate([x] * n, axis=1)


def _split_halves(q):
    lane = lax.broadcasted_iota(I32, q.shape, 1)
    lo = lane < DSA_HEAD_DIM
    return jnp.concatenate([jnp.where(lo, q, 0.0), jnp.where(lo, 0.0, q)], axis=0)


def _key_to_float(key):
    return pltpu.bitcast(jnp.where(key >= 0, key, key ^ jnp.int32(0x7FFFFFFF)), F32)


def _kth_largest(count_ge, rows, kk):
    def body(it, t):
        cand = t + jnp.left_shift(jnp.int32(1), 31 - it)
        return jnp.where(count_ge(_key_to_float(cand)) >= kk, cand, t)

    return _key_to_float(lax.fori_loop(0, 32, body, jnp.full((rows, 1), INT_MIN, I32)))


def _diff_attn_kernel(lam_ref, g_ref, q_ref, k_ref, v_ref, o_ref, m_scr, l_scr, acc_scr, *, blk, lam_init):
    i = pl.program_id(2)
    q2 = _split_halves(q_ref[0] * (DIFF_QK_DIM ** -0.5)).astype(BF16)
    m_scr[...] = jnp.full(m_scr.shape, NEG_INF, F32)
    l_scr[...] = jnp.zeros(l_scr.shape, F32)
    acc_scr[...] = jnp.zeros(acc_scr.shape, F32)

    def step(j, on_diagonal):
        start = pl.multiple_of(j * blk, blk)
        kj = k_ref[0, pl.ds(start, blk), :].astype(BF16)
        vj = v_ref[0, pl.ds(start, blk), :].astype(BF16)
        s = _dot_nt(q2, kj)
        if on_diagonal:
            row = lax.broadcasted_iota(I32, (2 * blk, blk), 0)
            col = lax.broadcasted_iota(I32, (2 * blk, blk), 1)
            s = jnp.where(col <= jnp.where(row >= blk, row - blk, row), s, NEG_INF)
        m_old = m_scr[...]
        m_new = jnp.maximum(m_old, jnp.max(s, axis=1, keepdims=True))
        alpha = jnp.exp(m_old - m_new)
        p = jnp.exp(s - _lane_tile(m_new, blk // LANES))
        l_scr[...] = alpha * l_scr[...] + jnp.sum(p, axis=1, keepdims=True)
        acc_scr[...] = alpha * acc_scr[...] + _dot(p.astype(BF16), vj)
        m_scr[...] = m_new

    def body(j, carry):
        step(j, False)
        return carry

    lax.fori_loop(0, i, body, 0)
    step(i, True)
    o = acc_scr[...] / l_scr[...]
    a = o[:blk] - _lambda(lam_ref, lam_init) * o[blk:]
    o_ref[0] = _rms(a, g_ref[...]) * (1.0 - lam_init)


def _diff_attn_prompt(q, k, v, lam_rows, g_head, lam_init, blk=512):
    b, t, _ = q.shape
    blk = min(blk, t)
    kern = functools.partial(_diff_attn_kernel, blk=blk, lam_init=lam_init)
    return pl.pallas_call(
        kern,
        grid=(b, DIFF_HEADS, t // blk),
        in_specs=[
            pl.BlockSpec(lam_rows.shape, lambda bi, h, i: (0, 0)),
            pl.BlockSpec((1, DIFF_V_DIM), lambda bi, h, i: (0, 0)),
            pl.BlockSpec((1, blk, LANES), lambda bi, h, i: (bi, i, h)),
            pl.BlockSpec((1, t, LANES), lambda bi, h, i: (bi, 0, h)),
            pl.BlockSpec((1, t, LANES), lambda bi, h, i: (bi, 0, h)),
        ],
        out_specs=pl.BlockSpec((1, blk, LANES), lambda bi, h, i: (bi, i, h)),
        out_shape=jax.ShapeDtypeStruct((b, t, HEAD_W), F32),
        scratch_shapes=[pltpu.VMEM((2 * blk, LANES), F32), pltpu.VMEM((2 * blk, LANES), F32),
                        pltpu.VMEM((2 * blk, LANES), F32)],
        compiler_params=_cparams(("parallel", "parallel", "parallel")),
    )(lam_rows, g_head.reshape(1, DIFF_V_DIM), q, k, v)


def _dsa_prompt_kernel(q_ref, iq_ref, tail_q_ref, k_ref, v_ref, tail_k_ref, o_ref, key_scr, bias_scr,
                       *, qb, t, topk, wstep):
    i = pl.program_id(1)

    def body(w):
        qpos = i * qb + lax.broadcasted_iota(I32, (qb, w), 0)
        kpos = lax.broadcasted_iota(I32, (qb, w), 1)
        causal = kpos <= qpos

        ik = tail_k_ref[0, :w, :IDX_DIM].astype(BF16)
        tq = tail_q_ref[0]
        score = jnp.zeros((qb, w), F32)
        for h in range(IDX_HEADS):
            qh = iq_ref[0, :, h * IDX_DIM:(h + 1) * IDX_DIM].astype(BF16)
            dots = _dot_nt(qh, ik) * (IDX_DIM ** -0.5)
            score = score + jnp.maximum(dots, 0.0) * tq[:, IDX_DIM + h:IDX_DIM + h + 1]
        key_scr[:, :w] = jnp.where(causal, score, NEG_INF)

        def count_ge(cand):
            return jnp.sum(jnp.where(key_scr[:, :w] >= cand, 1.0, 0.0), axis=1, keepdims=True)

        thr = _kth_largest(count_ge, qb, float(topk))
        key = key_scr[:, :w]
        above = (key > thr) | (qpos < topk)
        n_above = jnp.sum(jnp.where(above, 1.0, 0.0), axis=1, keepdims=True)
        room = float(topk) - n_above
        tie = key == thr
        tri = (lax.broadcasted_iota(I32, (LANES, LANES), 0) <= lax.broadcasted_iota(I32, (LANES, LANES), 1))
        tri = jnp.where(tri, 1.0, 0.0).astype(BF16)
        before = jnp.zeros((qb, 1), F32)
        for c in range(0, w, LANES):
            tc = tie[:, c:c + LANES]
            rank = _dot(jnp.where(tc, 1.0, 0.0).astype(BF16), tri) + before
            sel = (above[:, c:c + LANES] | (tc & (rank <= room))) & causal[:, c:c + LANES]
            bias_scr[:, c:c + LANES] = jnp.where(sel, 0.0, NEG_INF)
            before = rank[:, LANES - 1:LANES]

        lane = lax.broadcasted_iota(I32, (qb, LANES), 1)
        for pb in range(DSA_HEADS // 2):
            cols = slice(pb * LANES, (pb + 1) * LANES)
            q2 = _split_halves(q_ref[0, :, cols] * (DSA_HEAD_DIM ** -0.5)).astype(BF16)
            kp = k_ref[0, :w, cols].astype(BF16)
            vp = v_ref[0, :w, cols].astype(BF16)
            outs = []
            for m in range(2):
                s = _dot_nt(q2[m * qb:(m + 1) * qb], kp) + bias_scr[:, :w]
                mx = jnp.max(s, axis=1, keepdims=True)
                p = jnp.exp(s - mx)
                den = jnp.sum(p, axis=1, keepdims=True)
                outs.append(_dot(p.astype(BF16), vp) / den)
            o_ref[0, :, cols] = jnp.where(lane < DSA_HEAD_DIM, outs[0], outs[1])

    n_w = t // wstep
    needed = ((i + 1) * qb + wstep - 1) // wstep
    for a in range(1, n_w + 1):
        pl.when(needed == a)(functools.partial(body, a * wstep))


def _dsa_prompt(q, k, v, iq, tail, qb=256, wstep=512):
    b, t, _ = q.shape
    qb = min(qb, t)
    wstep = min(wstep, t)
    assert t % wstep == 0 and wstep % qb == 0
    topk = min(DSA_TOPK, t // 4)
    kern = functools.partial(_dsa_prompt_kernel, qb=qb, t=t, topk=topk, wstep=wstep)
    return pl.pallas_call(
        kern,
        grid=(b, t // qb),
        in_specs=[
            pl.BlockSpec((1, qb, HEAD_W), lambda bi, i: (bi, i, 0)),
            pl.BlockSpec((1, qb, IDX_HEADS * IDX_DIM), lambda bi, i: (bi, i, 0)),
            pl.BlockSpec((1, qb, LANES), lambda bi, i: (bi, i, 0)),
            pl.BlockSpec((1, t, HEAD_W), lambda bi, i: (bi, 0, 0)),
            pl.BlockSpec((1, t, HEAD_W), lambda bi, i: (bi, 0, 0)),
            pl.BlockSpec((1, t, LANES), lambda bi, i: (bi, 0, 0)),
        ],
        out_specs=pl.BlockSpec((1, qb, HEAD_W), lambda bi, i: (bi, i, 0)),
        out_shape=jax.ShapeDtypeStruct((b, t, HEAD_W), F32),
        scratch_shapes=[pltpu.VMEM((qb, t), F32), pltpu.VMEM((qb, t), F32)],
        compiler_params=_cparams(("parallel", "parallel")),
    )(q, iq, tail, k, v, tail)


def _dsa_sample_select_kernel(pt_ref, iq_ref, tail_ref, *refs, pg, tn, n_chunks, topk):
    page_refs = refs[:pg]
    bias_ref = refs[pg]
    key_scr = refs[pg + 1]
    j = pl.program_id(1)
    tail = tail_ref[0]
    iq = iq_ref[0]
    q_rows = jnp.concatenate([iq[:, h * IDX_DIM:(h + 1) * IDX_DIM] for h in range(IDX_HEADS)], axis=0).astype(BF16)
    w_rows = jnp.concatenate([tail[:, IDX_DIM + h:IDX_DIM + h + 1] for h in range(IDX_HEADS)], axis=0)

    def scores(dots):
        d = jnp.maximum(dots * (IDX_DIM ** -0.5), 0.0) * w_rows
        s = d[0:tn]
        for h in range(1, IDX_HEADS):
            s = s + d[h * tn:(h + 1) * tn]
        return s

    k_step = jnp.concatenate([page_refs[g][0, 0] for g in range(pg)], axis=1).astype(BF16)
    s_step = scores(_dot(q_rows, k_step))
    for g in range(pg):
        key_scr[j * pg + g] = s_step[:, g * LANES:(g + 1) * LANES]

    @pl.when(j == pl.num_programs(1) - 1)
    def _():
        n_past = n_chunks - 1
        k_new = jnp.concatenate([tail[:, :IDX_DIM], jnp.zeros((LANES - tn, IDX_DIM), F32)], axis=0).astype(BF16)
        s_new = scores(_dot_nt(q_rows, k_new))
        row = lax.broadcasted_iota(I32, (tn, LANES), 0)
        col = lax.broadcasted_iota(I32, (tn, LANES), 1)
        key_scr[n_past] = jnp.where(col <= row, s_new, NEG_INF)

        def count_ge(cand):
            n_part = min(4, n_past)
            per = n_past // n_part
            tot = jnp.where(key_scr[n_past] >= cand, 1.0, 0.0)
            for a in range(n_part):
                hi = n_past if a == n_part - 1 else (a + 1) * per
                tot = tot + jnp.sum(jnp.where(key_scr[a * per:hi] >= cand[None], 1.0, 0.0), axis=0)
            return jnp.sum(tot, axis=1, keepdims=True)

        thr = _kth_largest(count_ge, tn, float(topk))
        key = key_scr[...]
        n_valid = n_past * LANES + 1 + lax.broadcasted_iota(I32, (tn, 1), 0)
        above = (key > thr[None]) | (n_valid <= topk)[None]
        n_above = jnp.sum(jnp.sum(jnp.where(above, 1.0, 0.0), axis=0), axis=1, keepdims=True)
        room = float(topk) - n_above
        tie = jnp.where(key == thr[None], 1.0, 0.0)
        tri = (lax.broadcasted_iota(I32, (LANES, LANES), 0) <= lax.broadcasted_iota(I32, (LANES, LANES), 1))
        tri = jnp.where(tri, 1.0, 0.0).astype(BF16)
        within = _dot(tie.reshape(n_chunks * tn, LANES).astype(BF16), tri).reshape(n_chunks, tn, LANES)
        totals = within[:, :, LANES - 1:LANES]

        def chunk_body(c, before):
            rank = within[c] + before
            sel = above[c] | ((tie[c] > 0) & (rank <= room))
            bias_ref[0, c] = jnp.where(sel, 0.0, NEG_INF)
            return before + totals[c]

        before = jnp.zeros((tn, 1), F32)
        for c in range(n_chunks):
            before = chunk_body(c, before)
        bias_ref[0, n_past] = jnp.where(col <= row, bias_ref[0, n_past], NEG_INF)


def _dsa_sample_select(iq, tail, cache_ik_t, page_table, li, pg=32):
    bd, tn, _ = iq.shape
    n_pages = page_table.shape[1]
    page = cache_ik_t.shape[3]
    pg = min(pg, n_pages)
    assert page == LANES and n_pages % pg == 0
    n_chunks = n_pages + 1
    topk = min(DSA_TOPK, (n_pages * page + tn) // 4)
    kern = functools.partial(_dsa_sample_select_kernel, pg=pg, tn=tn, n_chunks=n_chunks, topk=topk)

    def page_spec(g):
        return pl.BlockSpec((1, 1, IDX_DIM, page), lambda b, j, pt: (li, pt[b, j * pg + g], 0, 0))

    return pl.pallas_call(
        kern,
        grid_spec=pltpu.PrefetchScalarGridSpec(
            num_scalar_prefetch=1,
            grid=(bd, n_pages // pg),
            in_specs=[
                pl.BlockSpec((1, tn, IDX_HEADS * IDX_DIM), lambda b, j, pt: (b, 0, 0)),
                pl.BlockSpec((1, tn, LANES), lambda b, j, pt: (b, 0, 0)),
            ] + [page_spec(g) for g in range(pg)],
            out_specs=pl.BlockSpec((1, n_chunks, tn, LANES), lambda b, j, pt: (b, 0, 0, 0)),
            scratch_shapes=[pltpu.VMEM((n_chunks, tn, LANES), F32)],
        ),
        out_shape=jax.ShapeDtypeStruct((bd, n_chunks, tn, LANES), F32),
        compiler_params=_cparams(("parallel", "arbitrary")),
    )(page_table, iq, tail, *([cache_ik_t] * pg))


def _paged_attn_kernel(pt_ref, lam_ref, g_ref, q_ref, kn_ref, vn_ref, bias_ref, ck_ref, cv_ref, o_ref,
                       kbuf, vbuf, sem, m_scr, l_scr, acc_scr, *, pg, tn, diff, lam_init, li):
    b = pl.program_id(0)
    j = pl.program_id(1)
    n_j = pl.num_programs(1)
    nblk = HEAD_W // LANES
    rows = 2 * tn
    scale = DIFF_QK_DIM ** -0.5

    step = b * n_j + j
    slot = step % 2

    def page_copies(bb, jj, sl):
        cps = []
        for g in range(pg):
            pid = pt_ref[bb, jj * pg + g]
            cps.append(pltpu.make_async_copy(ck_ref.at[li, pid], kbuf.at[sl, g], sem.at[sl, 0]))
            cps.append(pltpu.make_async_copy(cv_ref.at[li, pid], vbuf.at[sl, g], sem.at[sl, 1]))
        return cps

    @pl.when(step == 0)
    def _():
        for cp in page_copies(b, j, slot):
            cp.start()

    @pl.when(step + 1 < pl.num_programs(0) * n_j)
    def _():
        wrap = j + 1 == n_j
        for cp in page_copies(jnp.where(wrap, b + 1, b), jnp.where(wrap, 0, j + 1), 1 - slot):
            cp.start()

    for cp in page_copies(b, j, slot):
        cp.wait()

    def update(blk, s_parts, pv_fn):
        s = jnp.concatenate(s_parts, axis=1)
        m_old = m_scr[blk]
        m_new = jnp.maximum(m_old, jnp.max(s, axis=1, keepdims=True))
        alpha = jnp.exp(m_old - m_new)
        p = jnp.exp(s - _lane_tile(m_new, len(s_parts)))
        l_scr[blk] = alpha * l_scr[blk] + jnp.sum(p, axis=1, keepdims=True)
        pv = None
        for g in range(len(s_parts)):
            term = pv_fn(g, p[:, g * LANES:(g + 1) * LANES].astype(BF16))
            pv = term if pv is None else pv + term
        acc_scr[blk] = alpha * acc_scr[blk] + pv
        m_scr[blk] = m_new

    q_all = q_ref[0] * scale
    q2 = [_split_halves(q_all[:, b * LANES:(b + 1) * LANES]).astype(BF16) for b in range(nblk)]

    @pl.when(j == 0)
    def _():
        m_scr[...] = jnp.full(m_scr.shape, jnp.finfo(F32).min, F32)
        l_scr[...] = jnp.zeros(l_scr.shape, F32)
        acc_scr[...] = jnp.zeros(acc_scr.shape, F32)
        n_past = bias_ref.shape[1] - 1
        row = lax.broadcasted_iota(I32, (rows, LANES), 0)
        col = lax.broadcasted_iota(I32, (rows, LANES), 1)
        causal = col <= jnp.where(row >= tn, row - tn, row)
        bn = bias_ref[0, n_past]
        bn2 = jnp.concatenate([bn, bn], axis=0)
        pad = jnp.zeros((LANES - tn, LANES), F32)
        for blk in range(nblk):
            cols = slice(blk * LANES, (blk + 1) * LANES)
            kn = jnp.concatenate([kn_ref[0, :, cols], pad], axis=0).astype(BF16)
            vn = jnp.concatenate([vn_ref[0, :, cols], pad], axis=0).astype(BF16)
            s = jnp.where(causal, _dot_nt(q2[blk], kn), NEG_INF)
            if not diff:
                s = s + bn2
            update(blk, [s], lambda g, p: _dot(p, vn))

    if not diff:
        bias2 = []
        for g in range(pg):
            bp = bias_ref[0, j * pg + g]
            bias2.append(jnp.concatenate([bp, bp], axis=0))
    for blk in range(nblk):
        rows_blk = slice(blk * LANES, (blk + 1) * LANES)
        s_parts = [_dot(q2[blk], kbuf[slot, g, rows_blk, :].astype(BF16)) for g in range(pg)]
        if diff:
            update(blk, s_parts,
                   lambda g, p: _dot(p, vbuf[slot, g, pl.ds(blk, LANES, stride=nblk), :].astype(BF16)))
        else:
            update(blk, [s + b2 for s, b2 in zip(s_parts, bias2)],
                   lambda g, p: _dot_nt(p, vbuf[slot, g, rows_blk, :].astype(BF16)))

    @pl.when(j == pl.num_programs(1) - 1)
    def _():
        lane = lax.broadcasted_iota(I32, (tn, LANES), 1)
        for blk in range(nblk):
            o = acc_scr[blk] / l_scr[blk]
            if diff:
                a = o[:tn] - _lambda(lam_ref, lam_init) * o[tn:]
                r = _rms(a, g_ref[...]) * (1.0 - lam_init)
            else:
                r = jnp.where(lane < DSA_HEAD_DIM, o[:tn], o[tn:])
            o_ref[0, :, blk * LANES:(blk + 1) * LANES] = r


def _paged_attn(q, k_new, v_new, bias, cache_k, cache_v, page_table, li, lam_rows, g_head, lam_init, diff, pg=16):
    bd, tn, _ = q.shape
    n_pages = page_table.shape[1]
    pg = min(pg, n_pages)
    assert n_pages % pg == 0 and cache_k.shape[2:] == (HEAD_W, LANES) and cache_v.shape[2:] == (HEAD_W, LANES)
    n_chunks = n_pages + 1
    kern = functools.partial(_paged_attn_kernel, pg=pg, tn=tn, diff=diff, lam_init=lam_init, li=li)
    nblk = HEAD_W // LANES
    tok = pl.BlockSpec((1, tn, HEAD_W), lambda b, j, pt: (b, 0, 0))
    stats = pltpu.VMEM((nblk, 2 * tn, LANES), F32)
    pages = pltpu.VMEM((2, pg, HEAD_W, LANES), F32)
    return pl.pallas_call(
        kern,
        grid_spec=pltpu.PrefetchScalarGridSpec(
            num_scalar_prefetch=1,
            grid=(bd, n_pages // pg),
            in_specs=[
                pl.BlockSpec(lam_rows.shape, lambda b, j, pt: (0, 0)),
                pl.BlockSpec((1, DIFF_V_DIM), lambda b, j, pt: (0, 0)),
                tok, tok, tok,
                pl.BlockSpec((1, n_chunks, tn, LANES), lambda b, j, pt: (b, 0, 0, 0)),
                pl.BlockSpec(memory_space=pl.ANY),
                pl.BlockSpec(memory_space=pl.ANY),
            ],
            out_specs=pl.BlockSpec((1, tn, HEAD_W), lambda b, j, pt: (b, 0, 0)),
            scratch_shapes=[pages, pages, pltpu.SemaphoreType.DMA((2, 2)), stats, stats, stats],
        ),
        out_shape=jax.ShapeDtypeStruct((bd, tn, HEAD_W), F32),
        compiler_params=_cparams(("arbitrary", "arbitrary")),
    )(page_table, lam_rows, g_head.reshape(1, DIFF_V_DIM), q, k_new, v_new, bias, cache_k, cache_v)


def _mlstm_kernel(q_ref, k_ref, v_ref, og_ref, gcol_ref, grow_ref, bcol_ref, brow_ref, gh_ref, c0_ref, n0_ref, m0_ref,
                  h_ref, c_out, n_out, m_out, c_scr, n_scr, m_scr, *, chunk, n_sub, dk, dv):
    j = pl.program_id(1)
    nh = MLSTM_HEADS

    @pl.when(j == 0)
    def _():
        c_scr[...] = c0_ref[0]
        for h in range(nh):
            n_scr[h] = n0_ref[0, h:h + 1, :]
            m_scr[h] = m0_ref[0, :, h:h + 1]

    r_i = lax.broadcasted_iota(I32, (chunk, chunk), 0)
    c_i = lax.broadcasted_iota(I32, (chunk, chunk), 1)
    causal = c_i <= r_i

    def gates(x, is_input_gate):
        ig = IGATE_SOFTCAP * jnp.tanh(x / IGATE_SOFTCAP)
        lf = jnp.minimum(x, 0.0) - jnp.log1p(jnp.exp(-jnp.abs(x)))
        return jnp.where(is_input_gate, ig, lf)

    def chunk_body(s, carry):
        r0 = pl.multiple_of(s * chunk, chunk)
        gc = gcol_ref[0, s] + bcol_ref[...]
        gr = grow_ref[0, s] + brow_ref[...]
        gc = gates(gc, lax.broadcasted_iota(I32, gc.shape, 1) < nh)
        gr = gates(gr, lax.broadcasted_iota(I32, gr.shape, 0) < nh)
        for h in range(nh):
            i_col, f_col = gc[:, h:h + 1], gc[:, nh + h:nh + h + 1]
            i_row, f_row = gr[h:h + 1, :], gr[nh + h:nh + h + 1, :]
            b_col = jnp.sum(jnp.where(causal, f_row, 0.0), axis=1, keepdims=True)
            b_row = jnp.sum(jnp.where(r_i <= c_i, f_col, 0.0), axis=0, keepdims=True)
            m_prev = m_scr[h]
            c_prev = c_scr[h]
            n_prev = n_scr[h]
            qc = q_ref[0, pl.ds(r0, chunk), h * dk:(h + 1) * dk] * (dk ** -0.5)
            kc = k_ref[0, pl.ds(r0, chunk), h * dk:(h + 1) * dk]
            vc = v_ref[0, pl.ds(r0, chunk), h * dv:(h + 1) * dv]
            dmat = jnp.where(causal, b_col - b_row + i_row, NEG_INF)
            inter = b_col + m_prev
            mt = jnp.maximum(inter, jnp.max(dmat, axis=1, keepdims=True))
            sm = _dot_nt(qc.astype(BF16), kc.astype(BF16)) * jnp.exp(dmat - mt)
            w_inter = jnp.exp(inter - mt)
            num = _dot(sm.astype(BF16), vc.astype(BF16)) + w_inter * _dot_nt(qc.astype(BF16), c_prev.astype(BF16))
            den = jnp.sum(sm, axis=1, keepdims=True) + w_inter * jnp.sum(qc * n_prev, axis=1, keepdims=True)
            hh = num / jnp.maximum(jnp.abs(den), jnp.exp(-mt))
            b_end = jnp.sum(f_row, axis=1, keepdims=True)
            d_end = b_end - b_col + i_col
            m_new = jnp.maximum(b_end + m_prev, jnp.max(d_end, axis=0, keepdims=True))
            w_end = jnp.exp(d_end - m_new)
            decay = jnp.exp(b_end + m_prev - m_new)
            kw = kc * w_end
            c_scr[h] = decay * c_prev + _dot_tn(vc.astype(BF16), kw.astype(BF16))
            n_scr[h] = decay * n_prev + jnp.sum(kw, axis=0, keepdims=True)
            m_scr[h] = m_new
            hn = _rms(hh, gh_ref[:, h * dv:(h + 1) * dv])
            og = og_ref[0, pl.ds(r0, chunk), h * dv:(h + 1) * dv]
            h_ref[0, pl.ds(r0, chunk), h * dv:(h + 1) * dv] = hn * (1.0 / (1.0 + jnp.exp(-og)))
        return carry

    lax.fori_loop(0, n_sub, chunk_body, 0)

    @pl.when(j == pl.num_programs(1) - 1)
    def _():
        c_out[0] = c_scr[...]
        for h in range(nh):
            n_out[0, h:h + 1, :] = n_scr[h]
            m_out[0, :, h:h + 1] = m_scr[h]


def _mlstm(q, k, v, og, gates_raw, b_i, b_f, g_head, c0, n0, m0, tb=512):
    b, t, _ = q.shape
    nh = MLSTM_HEADS
    dk = q.shape[2] // nh
    dv = v.shape[2] // nh
    chunk = math.gcd(t, MLSTM_CHUNK)
    tb = min(tb, t)
    n_sub = tb // chunk
    nc = t // chunk
    g = gates_raw[:, :, :2 * nh].reshape(b, nc, chunk, 2 * nh)
    g_row = jnp.swapaxes(g, 2, 3)
    bias = jnp.concatenate([b_i, b_f]).astype(F32)
    kern = functools.partial(_mlstm_kernel, chunk=chunk, n_sub=n_sub, dk=dk, dv=dv)
    tok = lambda w: pl.BlockSpec((1, tb, w), lambda bi, j: (bi, j, 0))
    outs = pl.pallas_call(
        kern,
        grid=(b, t // tb),
        in_specs=[
            tok(nh * dk), tok(nh * dk), tok(nh * dv), tok(nh * dv),
            pl.BlockSpec((1, n_sub, chunk, 2 * nh), lambda bi, j: (bi, j, 0, 0)),
            pl.BlockSpec((1, n_sub, 2 * nh, chunk), lambda bi, j: (bi, j, 0, 0)),
            pl.BlockSpec((1, 2 * nh), lambda bi, j: (0, 0)),
            pl.BlockSpec((2 * nh, 1), lambda bi, j: (0, 0)),
            pl.BlockSpec((1, nh * dv), lambda bi, j: (0, 0)),
            pl.BlockSpec((1, nh, dv, dk), lambda bi, j: (bi, 0, 0, 0)),
            pl.BlockSpec((1, nh, dk), lambda bi, j: (bi, 0, 0)),
            pl.BlockSpec((1, 1, nh), lambda bi, j: (bi, 0, 0)),
        ],
        out_specs=[
            tok(nh * dv),
            pl.BlockSpec((1, nh, dv, dk), lambda bi, j: (bi, 0, 0, 0)),
            pl.BlockSpec((1, nh, dk), lambda bi, j: (bi, 0, 0)),
            pl.BlockSpec((1, 1, nh), lambda bi, j: (bi, 0, 0)),
        ],
        out_shape=[
            jax.ShapeDtypeStruct((b, t, nh * dv), F32),
            jax.ShapeDtypeStruct((b, nh, dv, dk), F32),
            jax.ShapeDtypeStruct((b, nh, dk), F32),
            jax.ShapeDtypeStruct((b, 1, nh), F32),
        ],
        scratch_shapes=[pltpu.VMEM((nh, dv, dk), F32), pltpu.VMEM((nh, 1, dk), F32), pltpu.VMEM((nh, 1, 1), F32)],
        compiler_params=_cparams(("parallel", "arbitrary")),
    )(q, k, v, og, g, g_row, bias.reshape(1, 2 * nh), bias.reshape(2 * nh, 1), g_head.reshape(1, nh * dv),
      c0, n0, m0.reshape(b, 1, nh))
    hh, c, n, m = outs
    return hh, c, n, m.reshape(b, nh)


_EVEN_SEGS = tuple([(i * HEAD_W, HEAD_W, mode) for i, mode in enumerate(("rope", "rope", "plain", "rope", "rope", "plain"))]
                   + [(6 * HEAD_W, IDX_HEADS * IDX_DIM, "rope"), (6 * HEAD_W + IDX_HEADS * IDX_DIM, LANES, "tail")])


def _pad_cols(w, mult=LANES):
    pad = (-w.shape[1]) % mult
    return jnp.pad(w, ((0, 0), (0, pad))) if pad else w


def _attn_layer(yp, ys, w_in, w_out, lam_rows, g_head, g_mix, lam_init, caches, page_table, li, dims):
    b, t, bd, tn, past = dims
    d = yp.shape[1]
    w_in = _pad_cols(w_in.astype(BF16))
    w_out = w_out.astype(BF16)
    cache_dk, cache_dv, cache_sk, cache_sv, cache_ik = caches

    cos, sin = _rope_tables(jnp.arange(t))
    aq, ak, av, bq, bk, bv, iq, tail, ak_t, bk_t, bv_t, ik_t = _norm_proj(
        yp, g_mix, w_in, _EVEN_SEGS, cos, sin, t_segs=(1, 4, 5, 7))
    r3 = lambda x: x.reshape(b, t, x.shape[-1])
    a_out = _diff_attn_prompt(r3(aq), r3(ak), r3(av), lam_rows, g_head, lam_init)
    b_out = _dsa_prompt(r3(bq), r3(bk), r3(bv), r3(iq), r3(tail))
    mix_p = [a_out.reshape(b * t, HEAD_W), b_out.reshape(b * t, HEAD_W)]
    new_p = (jnp.moveaxis(ak_t.reshape(b, DIFF_HEADS, 2, DIFF_QK_DIM, t), -1, 1),
             av.reshape(b, t, DIFF_HEADS, DIFF_V_DIM),
             jnp.moveaxis(bk_t.reshape(b, DSA_HEADS, DSA_HEAD_DIM, t), -1, 1),
             jnp.moveaxis(bv_t.reshape(b, DSA_HEADS, DSA_HEAD_DIM, t), -1, 1),
             jnp.moveaxis(ik_t, -1, 1))

    cos, sin = _rope_tables(jnp.tile(past + jnp.arange(tn), bd))
    aq, ak, av, bq, bk, bv, iq, tail = _norm_proj(ys, g_mix, w_in, _EVEN_SEGS, cos, sin)
    s3 = lambda x: x.reshape(bd, tn, x.shape[-1])
    npool, page = cache_dk.shape[1], cache_dk.shape[2]
    slot_last = lambda c: jnp.moveaxis(c, 2, -1).reshape(c.shape[0], npool, -1, page)
    bias = _dsa_sample_select(s3(iq), s3(tail), slot_last(cache_ik), page_table, li)
    dv_rows = cache_dv.reshape(cache_dv.shape[0], npool, page * DIFF_HEADS, DIFF_V_DIM)
    a_out = _paged_attn(s3(aq), s3(ak), s3(av), bias, slot_last(cache_dk), dv_rows, page_table, li,
                        lam_rows, g_head, lam_init, diff=True)
    b_out = _paged_attn(s3(bq), s3(bk), s3(bv), bias, slot_last(cache_sk), slot_last(cache_sv), page_table, li,
                        lam_rows, g_head, lam_init, diff=False)
    mix_s = [a_out.reshape(bd * tn, HEAD_W), b_out.reshape(bd * tn, HEAD_W)]
    new_s = (ak.reshape(bd, tn, DIFF_HEADS, 2, DIFF_QK_DIM), av.reshape(bd, tn, DIFF_HEADS, DIFF_V_DIM),
             bk.reshape(bd, tn, DSA_HEADS, DSA_HEAD_DIM), bv.reshape(bd, tn, DSA_HEADS, DSA_HEAD_DIM),
             tail.reshape(bd, tn, LANES)[:, :, :IDX_DIM])
    return mix_p, mix_s, w_out, new_p, new_s


def _mlstm_layer(y, bsz, t, w_in, b_i, b_f, g_head, g_mix, c0, n0, m0):
    d = y.shape[1]
    nh = MLSTM_HEADS
    dk = d // (2 * nh)
    dv = d // nh
    segs = ((0, nh * dk, "plain"), (nh * dk, nh * dk, "plain"), (2 * nh * dk, nh * dv, "plain"),
            (2 * nh * dk + nh * dv, nh * dv, "plain"), (2 * nh * dk + 2 * nh * dv, LANES, "plain"))
    q, k, v, og, gts = _norm_proj(y, g_mix, w_in, segs)
    r3 = lambda x: x.reshape(bsz, t, x.shape[-1])
    hh, c, n, m = _mlstm(r3(q), r3(k), r3(v), r3(og), r3(gts), b_i, b_f, g_head, c0, n0, m0)
    return [hh.reshape(bsz * t, nh * dv)], (c, n, m)


def kernel(x_prompt, x_sample, cache_diff_k, cache_diff_v, cache_dsa_k, cache_dsa_v, cache_idx_k, state_mlstm_c, state_mlstm_n, state_mlstm_m, page_table, g_mix, g_ffn, g_final, w_attn_in, w_attn_out, lambda_q1, lambda_k1, lambda_q2, lambda_k2, g_diff_head, w_mlstm_in, b_mlstm_i, b_mlstm_f, g_mlstm_head, w_mlstm_out, w_ffn_up, w_ffn_down):
    b, t, d = x_prompt.shape
    bd, tn, _ = x_sample.shape
    depth = g_mix.shape[0]
    past = page_table.shape[1] * cache_idx_k.shape[2]
    nh = MLSTM_HEADS
    yp = x_prompt.reshape(b * t, d)
    ys = x_sample.reshape(bd * tn, d)
    new_p, new_s, st_p, st_s = [], [], [], []
    for layer in range(depth):
        li = layer // 2
        last = layer == depth - 1
        if layer % 2 == 0:
            lam_init = 0.8 - 0.6 * math.exp(-0.3 * layer)
            lam_rows = jnp.stack([lambda_q1[li], lambda_k1[li], lambda_q2[li], lambda_k2[li]]).astype(F32)
            mix_p, mix_s, w_out, np_, ns_ = _attn_layer(
                yp, ys, w_attn_in[li], w_attn_out[li], lam_rows, g_diff_head[li], g_mix[layer], lam_init,
                (cache_diff_k, cache_diff_v, cache_dsa_k, cache_dsa_v, cache_idx_k), page_table, li,
                (b, t, bd, tn, past))
            new_p.append(np_)
            new_s.append(ns_)
        else:
            w_in = _pad_cols(w_mlstm_in[li].astype(BF16))
            w_out = w_mlstm_out[li].astype(BF16)
            dk = d // (2 * nh)
            dv = d // nh
            zc = jnp.zeros((b, nh, dv, dk), F32)
            zn = jnp.zeros((b, nh, dk), F32)
            zm = jnp.zeros((b, nh), F32)
            mix_p, sp = _mlstm_layer(yp, b, t, w_in, b_mlstm_i[li], b_mlstm_f[li], g_mlstm_head[li],
                                     g_mix[layer], zc, zn, zm)
            mix_s, ss = _mlstm_layer(ys, bd, tn, w_in, b_mlstm_i[li], b_mlstm_f[li], g_mlstm_head[li],
                                     g_mix[layer], state_mlstm_c[li], state_mlstm_n[li], state_mlstm_m[li])
            st_p.append(sp)
            st_s.append(ss)
        w_up = w_ffn_up[layer].astype(BF16)
        w_down = w_ffn_down[layer].astype(BF16)
        gf = g_final if last else None
        yp = _mix_ffn(mix_p, w_out, yp, g_ffn[layer], w_up, w_down, gf)
        ys = _mix_ffn(mix_s, w_out, ys, g_ffn[layer], w_up, w_down, gf)
    stack = lambda items, idx: jnp.stack([it[idx] for it in items])
    return (yp.reshape(b, t, d), ys.reshape(bd, tn, d),
            stack(new_p, 0), stack(new_p, 1), stack(new_p, 2), stack(new_p, 3), stack(new_p, 4),
            stack(st_p, 0), stack(st_p, 1), stack(st_p, 2),
            stack(new_s, 0), stack(new_s, 1), stack(new_s, 2), stack(new_s, 3), stack(new_s, 4),
            stack(st_s, 0), stack(st_s, 1), stack(st_s, 2))
```

```python
import functools
import math

import jax
import jax.numpy as jnp
from jax import lax
from jax.experimental import pallas as pl
from jax.experimental.pallas import tpu as pltpu

F32 = jnp.float32
BF16 = jnp.bfloat16
I32 = jnp.int32

DIFF_HEADS = 4
DIFF_QK_DIM = 64
DIFF_V_DIM = 128
DSA_HEADS = 8
DSA_HEAD_DIM = 64
IDX_HEADS = 4
IDX_DIM = 64
DSA_TOPK = 256
MLSTM_HEADS = 4
MLSTM_CHUNK = 64
IGATE_SOFTCAP = 15.0
ROPE_THETA = 10000.0
RMS_EPS = 1e-6
HEAD_W = DIFF_HEADS * 2 * DIFF_QK_DIM

LANES = 128
SUBLANES = 8
VMEM_LIMIT = 56 * 1024 * 1024

INT_MIN = -(2 ** 31)
NEG_INF = float("-inf")


def _cparams(sem):
    return pltpu.CompilerParams(dimension_semantics=sem, vmem_limit_bytes=VMEM_LIMIT)


def _rms(x, g):
    return x * lax.rsqrt(jnp.mean(x * x, axis=-1, keepdims=True) + RMS_EPS) * g


def _dot(a, b):
    return jnp.dot(a, b, preferred_element_type=F32)


def _dot_nt(a, b):
    return lax.dot_general(a, b, (((1,), (1,)), ((), ())), preferred_element_type=F32)


def _dot_tn(a, b):
    return lax.dot_general(a, b, (((0,), (0,)), ((), ())), preferred_element_type=F32)


def _norm_proj_kernel(x_ref, g_ref, w_ref, cos_ref, sin_ref, *out_refs, segs, t_segs, tm):
    x = x_ref[...]
    h = _rms(x, g_ref[...]).astype(BF16)
    lane = lax.broadcasted_iota(I32, (tm, LANES), 1)
    first_half = (lane & (IDX_DIM // 2)) == 0
    t_refs = dict(zip(t_segs, out_refs[len(segs):]))
    for si, ((c0, width, mode), o_ref) in enumerate(zip(segs, out_refs)):
        y = _dot(h, w_ref[:, c0:c0 + width])
        for c in range(0, width, LANES):
            r = y[:, c:c + LANES]
            if mode != "plain":
                partner = jnp.where(first_half, pltpu.roll(r, LANES - 32, 1), pltpu.roll(r, 32, 1))
                rot = r * cos_ref[...] + partner * sin_ref[...]
                if mode == "tail":
                    rot = jnp.where(lane < IDX_DIM, rot, r * (IDX_HEADS ** -0.5))
                r = rot
            o_ref[:, c:c + LANES] = r
            if si in t_refs:
                rt = r.T
                if mode == "tail":
                    t_refs[si][0] = rt[:IDX_DIM]
                else:
                    t_refs[si][0, c:c + LANES, :] = rt


def _norm_proj(x, g, w, segs, cos=None, sin=None, t_segs=(), tm=512):
    n, d = x.shape
    tm = min(tm, n)
    assert n % tm == 0
    if cos is None:
        cos = jnp.zeros((tm, LANES), F32)
        sin = cos
    p = cos.shape[0]
    assert p % tm == 0 and n % p == 0
    nper = p // tm
    kern = functools.partial(_norm_proj_kernel, segs=tuple(segs), t_segs=tuple(t_segs), tm=tm)
    t_width = lambda si: IDX_DIM if segs[si][2] == "tail" else segs[si][1]
    return pl.pallas_call(
        kern,
        grid=(n // tm,),
        in_specs=[
            pl.BlockSpec((tm, d), lambda i: (i, 0)),
            pl.BlockSpec((1, d), lambda i: (0, 0)),
            pl.BlockSpec(w.shape, lambda i: (0, 0)),
            pl.BlockSpec((tm, LANES), lambda i: (i % nper, 0)),
            pl.BlockSpec((tm, LANES), lambda i: (i % nper, 0)),
        ],
        out_specs=[pl.BlockSpec((tm, wd), lambda i: (i, 0)) for (_, wd, _) in segs]
        + [pl.BlockSpec((1, t_width(si), tm), lambda i: (i // nper, 0, i % nper)) for si in t_segs],
        out_shape=[jax.ShapeDtypeStruct((n, wd), F32) for (_, wd, _) in segs]
        + [jax.ShapeDtypeStruct((n // p, t_width(si), p), F32) for si in t_segs],
        compiler_params=_cparams(("parallel",)),
    )(x, g.reshape(1, d), w, cos, sin)


def _rope_tables(pos):
    half = IDX_DIM // 2
    inv = ROPE_THETA ** (-jnp.arange(half, dtype=F32) / half)
    ang = pos.astype(F32)[:, None] * inv[None, :]
    cos = jnp.cos(ang)
    sin = jnp.sin(ang)
    cos_t = jnp.tile(cos, (1, LANES // half))
    sin_t = jnp.tile(jnp.concatenate([-sin, sin], axis=1), (1, LANES // IDX_DIM))
    return cos_t, sin_t


def _mix_ffn_kernel(*refs, widths, final_norm):
    n_in = len(widths)
    x_refs = refs[:n_in]
    wo_ref, res_ref, g_ref, wu_ref, wd_ref, gf_ref, o_ref, h_scr, acc_scr = refs[n_in:]
    f = pl.program_id(1)

    @pl.when(f == 0)
    def _():
        y = res_ref[...]
        r0 = 0
        for x_ref, wd in zip(x_refs, widths):
            y = y + _dot(x_ref[...].astype(BF16), wo_ref[r0:r0 + wd, :])
            r0 += wd
        h_scr[...] = _rms(y, g_ref[...]).astype(BF16)
        acc_scr[...] = y

    u = jnp.maximum(_dot(h_scr[...], wu_ref[...]), 0.0)
    acc_scr[...] += _dot((u * u).astype(BF16), wd_ref[...])

    @pl.when(f == pl.num_programs(1) - 1)
    def _():
        y = acc_scr[...]
        if final_norm:
            y = _rms(y, gf_ref[...])
        o_ref[...] = y


def _mix_ffn(xs, w_out, res, g, w_up, w_down, g_final=None, tm=512, tf=2048):
    n, d = res.shape
    dff = w_up.shape[1]
    tm = min(tm, n)
    widths = tuple(x.shape[1] for x in xs)
    final_norm = g_final is not None
    if g_final is None:
        g_final = g
    kern = functools.partial(_mix_ffn_kernel, widths=widths, final_norm=final_norm)
    row = lambda wd: pl.BlockSpec((tm, wd), lambda i, f: (i, 0))
    vec = pl.BlockSpec((1, d), lambda i, f: (0, 0))
    return pl.pallas_call(
        kern,
        grid=(n // tm, dff // tf),
        in_specs=[row(wd) for wd in widths] + [
            pl.BlockSpec(w_out.shape, lambda i, f: (0, 0)),
            row(d),
            vec,
            pl.BlockSpec((d, tf), lambda i, f: (0, f)),
            pl.BlockSpec((tf, d), lambda i, f: (f, 0)),
            vec,
        ],
        out_specs=row(d),
        out_shape=jax.ShapeDtypeStruct((n, d), F32),
        scratch_shapes=[pltpu.VMEM((tm, d), BF16), pltpu.VMEM((tm, d), F32)],
        compiler_params=_cparams(("parallel", "arbitrary")),
    )(*xs, w_out, res, g.reshape(1, d), w_up, w_down, g_final.reshape(1, d))


def _lambda(lam_ref, lam_init):
    l = lam_ref[...]
    a = jnp.sum(l[0:1] * l[1:2], axis=1, keepdims=True)
    b = jnp.sum(l[2:3] * l[3:4], axis=1, keepdims=True)
    return jnp.exp(a) - jnp.exp(b) + lam_init


def _lane_tile(x, n):
    return x if n == 1 else jnp.concatenate([x] * n, axis=1)


def _split_halves(q):
    lane = lax.broadcasted_iota(I32, q.shape, 1)
    lo = lane < DSA_HEAD_DIM
    return jnp.concatenate([jnp.where(lo, q, 0.0), jnp.where(lo, 0.0, q)], axis=0)


def _key_to_float(key):
    return pltpu.bitcast(jnp.where(key >= 0, key, key ^ jnp.int32(0x7FFFFFFF)), F32)


def _kth_largest(count_ge, rows, kk):
    def body(it, t):
        cand = t + jnp.left_shift(jnp.int32(1), 31 - it)
        return jnp.where(count_ge(_key_to_float(cand)) >= kk, cand, t)

    return _key_to_float(lax.fori_loop(0, 32, body, jnp.full((rows, 1), INT_MIN, I32)))


def _diff_attn_kernel(lam_ref, g_ref, q_ref, k_ref, v_ref, o_ref, m_scr, l_scr, acc_scr, *, blk, lam_init):
    i = pl.program_id(2)
    q2 = _split_halves(q_ref[0] * (DIFF_QK_DIM ** -0.5)).astype(BF16)
    m_scr[...] = jnp.full(m_scr.shape, NEG_INF, F32)
    l_scr[...] = jnp.zeros(l_scr.shape, F32)
    acc_scr[...] = jnp.zeros(acc_scr.shape, F32)

    def step(j, on_diagonal):
        start = pl.multiple_of(j * blk, blk)
        kj = k_ref[0, pl.ds(start, blk), :].astype(BF16)
        vj = v_ref[0, pl.ds(start, blk), :].astype(BF16)
        s = _dot_nt(q2, kj)
        if on_diagonal:
            row = lax.broadcasted_iota(I32, (2 * blk, blk), 0)
            col = lax.broadcasted_iota(I32, (2 * blk, blk), 1)
            s = jnp.where(col <= jnp.where(row >= blk, row - blk, row), s, NEG_INF)
        m_old = m_scr[...]
        m_new = jnp.maximum(m_old, jnp.max(s, axis=1, keepdims=True))
        alpha = jnp.exp(m_old - m_new)
        p = jnp.exp(s - _lane_tile(m_new, blk // LANES))
        l_scr[...] = alpha * l_scr[...] + jnp.sum(p, axis=1, keepdims=True)
        acc_scr[...] = alpha * acc_scr[...] + _dot(p.astype(BF16), vj)
        m_scr[...] = m_new

    def body(j, carry):
        step(j, False)
        return carry

    lax.fori_loop(0, i, body, 0)
    step(i, True)
    o = acc_scr[...] / l_scr[...]
    a = o[:blk] - _lambda(lam_ref, lam_init) * o[blk:]
    o_ref[0] = _rms(a, g_ref[...]) * (1.0 - lam_init)


def _diff_attn_prompt(q, k, v, lam_rows, g_head, lam_init, blk=512):
    b, t, _ = q.shape
    blk = min(blk, t)
    kern = functools.partial(_diff_attn_kernel, blk=blk, lam_init=lam_init)
    return pl.pallas_call(
        kern,
        grid=(b, DIFF_HEADS, t // blk),
        in_specs=[
            pl.BlockSpec(lam_rows.shape, lambda bi, h, i: (0, 0)),
            pl.BlockSpec((1, DIFF_V_DIM), lambda bi, h, i: (0, 0)),
            pl.BlockSpec((1, blk, LANES), lambda bi, h, i: (bi, i, h)),
            pl.BlockSpec((1, t, LANES), lambda bi, h, i: (bi, 0, h)),
            pl.BlockSpec((1, t, LANES), lambda bi, h, i: (bi, 0, h)),
        ],
        out_specs=pl.BlockSpec((1, blk, LANES), lambda bi, h, i: (bi, i, h)),
        out_shape=jax.ShapeDtypeStruct((b, t, HEAD_W), F32),
        scratch_shapes=[pltpu.VMEM((2 * blk, LANES), F32), pltpu.VMEM((2 * blk, LANES), F32),
                        pltpu.VMEM((2 * blk, LANES), F32)],
        compiler_params=_cparams(("parallel", "parallel", "parallel")),
    )(lam_rows, g_head.reshape(1, DIFF_V_DIM), q, k, v)


def _dsa_prompt_kernel(q_ref, iq_ref, tail_q_ref, k_ref, v_ref, tail_k_ref, o_ref, key_scr, bias_scr,
                       *, qb, t, topk, wstep):
    i = pl.program_id(1)

    def body(w):
        qpos = i * qb + lax.broadcasted_iota(I32, (qb, w), 0)
        kpos = lax.broadcasted_iota(I32, (qb, w), 1)
        causal = kpos <= qpos

        ik = tail_k_ref[0, :w, :IDX_DIM].astype(BF16)
        tq = tail_q_ref[0]
        score = jnp.zeros((qb, w), F32)
        for h in range(IDX_HEADS):
            qh = iq_ref[0, :, h * IDX_DIM:(h + 1) * IDX_DIM].astype(BF16)
            dots = _dot_nt(qh, ik) * (IDX_DIM ** -0.5)
            score = score + jnp.maximum(dots, 0.0) * tq[:, IDX_DIM + h:IDX_DIM + h + 1]
        key_scr[:, :w] = jnp.where(causal, score, NEG_INF)

        def count_ge(cand):
            return jnp.sum(jnp.where(key_scr[:, :w] >= cand, 1.0, 0.0), axis=1, keepdims=True)

        thr = _kth_largest(count_ge, qb, float(topk))
        key = key_scr[:, :w]
        above = (key > thr) | (qpos < topk)
        n_above = jnp.sum(jnp.where(above, 1.0, 0.0), axis=1, keepdims=True)
        room = float(topk) - n_above
        tie = key == thr
        tri = (lax.broadcasted_iota(I32, (LANES, LANES), 0) <= lax.broadcasted_iota(I32, (LANES, LANES), 1))
        tri = jnp.where(tri, 1.0, 0.0).astype(BF16)
        before = jnp.zeros((qb, 1), F32)
        for c in range(0, w, LANES):
            tc = tie[:, c:c + LANES]
            rank = _dot(jnp.where(tc, 1.0, 0.0).astype(BF16), tri) + before
            sel = (above[:, c:c + LANES] | (tc & (rank <= room))) & causal[:, c:c + LANES]
            bias_scr[:, c:c + LANES] = jnp.where(sel, 0.0, NEG_INF)
            before = rank[:, LANES - 1:LANES]

        lane = lax.broadcasted_iota(I32, (qb, LANES), 1)
        for pb in range(DSA_HEADS // 2):
            cols = slice(pb * LANES, (pb + 1) * LANES)
            q2 = _split_halves(q_ref[0, :, cols] * (DSA_HEAD_DIM ** -0.5)).astype(BF16)
            kp = k_ref[0, :w, cols].astype(BF16)
            vp = v_ref[0, :w, cols].astype(BF16)
            outs = []
            for m in range(2):
                s = _dot_nt(q2[m * qb:(m + 1) * qb], kp) + bias_scr[:, :w]
                mx = jnp.max(s, axis=1, keepdims=True)
                p = jnp.exp(s - mx)
                den = jnp.sum(p, axis=1, keepdims=True)
                outs.append(_dot(p.astype(BF16), vp) / den)
            o_ref[0, :, cols] = jnp.where(lane < DSA_HEAD_DIM, outs[0], outs[1])

    n_w = t // wstep
    needed = ((i + 1) * qb + wstep - 1) // wstep
    for a in range(1, n_w + 1):
        pl.when(needed == a)(functools.partial(body, a * wstep))


def _dsa_prompt(q, k, v, iq, tail, qb=256, wstep=512):
    b, t, _ = q.shape
    qb = min(qb, t)
    wstep = min(wstep, t)
    assert t % wstep == 0 and wstep % qb == 0
    topk = min(DSA_TOPK, t // 4)
    kern = functools.partial(_dsa_prompt_kernel, qb=qb, t=t, topk=topk, wstep=wstep)
    return pl.pallas_call(
        kern,
        grid=(b, t // qb),
        in_specs=[
            pl.BlockSpec((1, qb, HEAD_W), lambda bi, i: (bi, i, 0)),
            pl.BlockSpec((1, qb, IDX_HEADS * IDX_DIM), lambda bi, i: (bi, i, 0)),
            pl.BlockSpec((1, qb, LANES), lambda bi, i: (bi, i, 0)),
            pl.BlockSpec((1, t, HEAD_W), lambda bi, i: (bi, 0, 0)),
            pl.BlockSpec((1, t, HEAD_W), lambda bi, i: (bi, 0, 0)),
            pl.BlockSpec((1, t, LANES), lambda bi, i: (bi, 0, 0)),
        ],
        out_specs=pl.BlockSpec((1, qb, HEAD_W), lambda bi, i: (bi, i, 0)),
        out_shape=jax.ShapeDtypeStruct((b, t, HEAD_W), F32),
        scratch_shapes=[pltpu.VMEM((qb, t), F32), pltpu.VMEM((qb, t), F32)],
        compiler_params=_cparams(("parallel", "parallel")),
    )(q, iq, tail, k, v, tail)


def _fetch_pages(pt_ref, li, pg, streams, sem):
    b = pl.program_id(0)
    j = pl.program_id(1)
    n_j = pl.num_programs(1)
    step = b * n_j + j
    slot = step % 2

    def page_copies(bb, jj, sl):
        cps = []
        for g in range(pg):
            pid = pt_ref[bb, jj * pg + g]
            for k, (src, buf) in enumerate(streams):
                cps.append(pltpu.make_async_copy(src.at[li, pid], buf.at[sl, g], sem.at[sl, k]))
        return cps

    def start_all(cps):
        for n, cp in enumerate(cps):
            cp.start(priority=n % 2)

    @pl.when(step == 0)
    def _():
        start_all(page_copies(b, j, slot))

    @pl.when(step + 1 < pl.num_programs(0) * n_j)
    def _():
        wrap = j + 1 == n_j
        start_all(page_copies(jnp.where(wrap, b + 1, b), jnp.where(wrap, 0, j + 1), 1 - slot))

    for cp in page_copies(b, j, slot):
        cp.wait()
    return slot


def _dsa_sample_select_kernel(pt_ref, iq_ref, tail_ref, cik_ref, bias_ref, pbuf, sem, key_scr,
                              *, pg, tn, n_chunks, topk, li):
    j = pl.program_id(1)
    slot = _fetch_pages(pt_ref, li, pg, [(cik_ref, pbuf)], sem)
    tail = tail_ref[0]
    iq = iq_ref[0]
    q_rows = jnp.concatenate([iq[:, h * IDX_DIM:(h + 1) * IDX_DIM] for h in range(IDX_HEADS)], axis=0).astype(BF16)
    w_rows = jnp.concatenate([tail[:, IDX_DIM + h:IDX_DIM + h + 1] for h in range(IDX_HEADS)], axis=0)

    def scores(dots):
        d = jnp.maximum(dots * (IDX_DIM ** -0.5), 0.0) * w_rows
        s = d[0:tn]
        for h in range(1, IDX_HEADS):
            s = s + d[h * tn:(h + 1) * tn]
        return s

    k_step = jnp.concatenate([pbuf[slot, g] for g in range(pg)], axis=1).astype(BF16)
    s_step = scores(_dot(q_rows, k_step))
    for g in range(pg):
        key_scr[j * pg + g] = s_step[:, g * LANES:(g + 1) * LANES]

    @pl.when(j == pl.num_programs(1) - 1)
    def _():
        n_past = n_chunks - 1
        k_new = jnp.concatenate([tail[:, :IDX_DIM], jnp.zeros((LANES - tn, IDX_DIM), F32)], axis=0).astype(BF16)
        s_new = scores(_dot_nt(q_rows, k_new))
        row = lax.broadcasted_iota(I32, (tn, LANES), 0)
        col = lax.broadcasted_iota(I32, (tn, LANES), 1)
        key_scr[n_past] = jnp.where(col <= row, s_new, NEG_INF)

        def count_ge(cand):
            n_part = min(4, n_past)
            per = n_past // n_part
            tot = jnp.where(key_scr[n_past] >= cand, 1.0, 0.0)
            for a in range(n_part):
                hi = n_past if a == n_part - 1 else (a + 1) * per
                tot = tot + jnp.sum(jnp.where(key_scr[a * per:hi] >= cand[None], 1.0, 0.0), axis=0)
            return jnp.sum(tot, axis=1, keepdims=True)

        thr = _kth_largest(count_ge, tn, float(topk))
        key = key_scr[...]
        n_valid = n_past * LANES + 1 + lax.broadcasted_iota(I32, (tn, 1), 0)
        above = (key > thr[None]) | (n_valid <= topk)[None]
        n_above = jnp.sum(jnp.sum(jnp.where(above, 1.0, 0.0), axis=0), axis=1, keepdims=True)
        room = float(topk) - n_above
        tie = jnp.where(key == thr[None], 1.0, 0.0)
        tri = (lax.broadcasted_iota(I32, (LANES, LANES), 0) <= lax.broadcasted_iota(I32, (LANES, LANES), 1))
        tri = jnp.where(tri, 1.0, 0.0).astype(BF16)
        within = _dot(tie.reshape(n_chunks * tn, LANES).astype(BF16), tri).reshape(n_chunks, tn, LANES)
        totals = within[:, :, LANES - 1:LANES]

        def chunk_body(c, before):
            rank = within[c] + before
            sel = above[c] | ((tie[c] > 0) & (rank <= room))
            bias_ref[0, c] = jnp.where(sel, 0.0, NEG_INF)
            return before + totals[c]

        before = jnp.zeros((tn, 1), F32)
        for c in range(n_chunks):
            before = chunk_body(c, before)
        bias_ref[0, n_past] = jnp.where(col <= row, bias_ref[0, n_past], NEG_INF)


def _dsa_sample_select(iq, tail, cache_ik_t, page_table, li, pg=32):
    bd, tn, _ = iq.shape
    n_pages = page_table.shape[1]
    page = cache_ik_t.shape[3]
    pg = min(pg, n_pages)
    assert page == LANES and n_pages % pg == 0
    n_chunks = n_pages + 1
    topk = min(DSA_TOPK, (n_pages * page + tn) // 4)
    kern = functools.partial(_dsa_sample_select_kernel, pg=pg, tn=tn, n_chunks=n_chunks, topk=topk, li=li)
    return pl.pallas_call(
        kern,
        grid_spec=pltpu.PrefetchScalarGridSpec(
            num_scalar_prefetch=1,
            grid=(bd, n_pages // pg),
            in_specs=[
                pl.BlockSpec((1, tn, IDX_HEADS * IDX_DIM), lambda b, j, pt: (b, 0, 0)),
                pl.BlockSpec((1, tn, LANES), lambda b, j, pt: (b, 0, 0)),
                pl.BlockSpec(memory_space=pl.ANY),
            ],
            out_specs=pl.BlockSpec((1, n_chunks, tn, LANES), lambda b, j, pt: (b, 0, 0, 0)),
            scratch_shapes=[pltpu.VMEM((2, pg, IDX_DIM, page), F32), pltpu.SemaphoreType.DMA((2, 1)),
                            pltpu.VMEM((n_chunks, tn, LANES), F32)],
        ),
        out_shape=jax.ShapeDtypeStruct((bd, n_chunks, tn, LANES), F32),
        compiler_params=_cparams(("arbitrary", "arbitrary")),
    )(page_table, iq, tail, cache_ik_t)


def _paged_attn_kernel(pt_ref, lam_ref, g_ref, q_ref, kn_ref, vn_ref, bias_ref, ck_ref, cv_ref, o_ref,
                       kbuf, vbuf, sem, m_scr, l_scr, acc_scr, *, pg, tn, diff, lam_init, li):
    j = pl.program_id(1)
    nblk = HEAD_W // LANES
    rows = 2 * tn
    scale = DIFF_QK_DIM ** -0.5
    slot = _fetch_pages(pt_ref, li, pg, [(ck_ref, kbuf), (cv_ref, vbuf)], sem)

    def update(blk, s_parts, pv_fn):
        s = jnp.concatenate(s_parts, axis=1)
        m_old = m_scr[blk]
        m_new = jnp.maximum(m_old, jnp.max(s, axis=1, keepdims=True))
        alpha = jnp.exp(m_old - m_new)
        p = jnp.exp(s - _lane_tile(m_new, len(s_parts)))
        l_scr[blk] = alpha * l_scr[blk] + jnp.sum(p, axis=1, keepdims=True)
        pv = None
        for g in range(len(s_parts)):
            term = pv_fn(g, p[:, g * LANES:(g + 1) * LANES].astype(BF16))
            pv = term if pv is None else pv + term
        acc_scr[blk] = alpha * acc_scr[blk] + pv
        m_scr[blk] = m_new

    q_all = q_ref[0] * scale
    q2 = [_split_halves(q_all[:, b * LANES:(b + 1) * LANES]).astype(BF16) for b in range(nblk)]

    @pl.when(j == 0)
    def _():
        m_scr[...] = jnp.full(m_scr.shape, jnp.finfo(F32).min, F32)
        l_scr[...] = jnp.zeros(l_scr.shape, F32)
        acc_scr[...] = jnp.zeros(acc_scr.shape, F32)
        n_past = bias_ref.shape[1] - 1
        row = lax.broadcasted_iota(I32, (rows, LANES), 0)
        col = lax.broadcasted_iota(I32, (rows, LANES), 1)
        causal = col <= jnp.where(row >= tn, row - tn, row)
        bn = bias_ref[0, n_past]
        bn2 = jnp.concatenate([bn, bn], axis=0)
        pad = jnp.zeros((LANES - tn, LANES), F32)
        for blk in range(nblk):
            cols = slice(blk * LANES, (blk + 1) * LANES)
            kn = jnp.concatenate([kn_ref[0, :, cols], pad], axis=0).astype(BF16)
            vn = jnp.concatenate([vn_ref[0, :, cols], pad], axis=0).astype(BF16)
            s = jnp.where(causal, _dot_nt(q2[blk], kn), NEG_INF)
            if not diff:
                s = s + bn2
            update(blk, [s], lambda g, p: _dot(p, vn))

    if not diff:
        bias2 = []
        for g in range(pg):
            bp = bias_ref[0, j * pg + g]
            bias2.append(jnp.concatenate([bp, bp], axis=0))
    for blk in range(nblk):
        rows_blk = slice(blk * LANES, (blk + 1) * LANES)
        s_parts = [_dot(q2[blk], kbuf[slot, g, rows_blk, :].astype(BF16)) for g in range(pg)]
        if diff:
            update(blk, s_parts,
                   lambda g, p: _dot(p, vbuf[slot, g, pl.ds(blk, LANES, stride=nblk), :].astype(BF16)))
        else:
            update(blk, [s + b2 for s, b2 in zip(s_parts, bias2)],
                   lambda g, p: _dot_nt(p, vbuf[slot, g, rows_blk, :].astype(BF16)))

    @pl.when(j == pl.num_programs(1) - 1)
    def _():
        lane = lax.broadcasted_iota(I32, (tn, LANES), 1)
        for blk in range(nblk):
            o = acc_scr[blk] / l_scr[blk]
            if diff:
                a = o[:tn] - _lambda(lam_ref, lam_init) * o[tn:]
                r = _rms(a, g_ref[...]) * (1.0 - lam_init)
            else:
                r = jnp.where(lane < DSA_HEAD_DIM, o[:tn], o[tn:])
            o_ref[0, :, blk * LANES:(blk + 1) * LANES] = r


def _paged_attn(q, k_new, v_new, bias, cache_k, cache_v, page_table, li, lam_rows, g_head, lam_init, diff, pg=16):
    bd, tn, _ = q.shape
    n_pages = page_table.shape[1]
    pg = min(pg, n_pages)
    assert n_pages % pg == 0 and cache_k.shape[2:] == (HEAD_W, LANES) and cache_v.shape[2:] == (HEAD_W, LANES)
    n_chunks = n_pages + 1
    kern = functools.partial(_paged_attn_kernel, pg=pg, tn=tn, diff=diff, lam_init=lam_init, li=li)
    nblk = HEAD_W // LANES
    tok = pl.BlockSpec((1, tn, HEAD_W), lambda b, j, pt: (b, 0, 0))
    stats = pltpu.VMEM((nblk, 2 * tn, LANES), F32)
    pages = pltpu.VMEM((2, pg, HEAD_W, LANES), F32)
    return pl.pallas_call(
        kern,
        grid_spec=pltpu.PrefetchScalarGridSpec(
            num_scalar_prefetch=1,
            grid=(bd, n_pages // pg),
            in_specs=[
                pl.BlockSpec(lam_rows.shape, lambda b, j, pt: (0, 0)),
                pl.BlockSpec((1, DIFF_V_DIM), lambda b, j, pt: (0, 0)),
                tok, tok, tok,
                pl.BlockSpec((1, n_chunks, tn, LANES), lambda b, j, pt: (b, 0, 0, 0)),
                pl.BlockSpec(memory_space=pl.ANY),
                pl.BlockSpec(memory_space=pl.ANY),
            ],
            out_specs=pl.BlockSpec((1, tn, HEAD_W), lambda b, j, pt: (b, 0, 0)),
            scratch_shapes=[pages, pages, pltpu.SemaphoreType.DMA((2, 2)), stats, stats, stats],
        ),
        out_shape=jax.ShapeDtypeStruct((bd, tn, HEAD_W), F32),
        compiler_params=_cparams(("arbitrary", "arbitrary")),
    )(page_table, lam_rows, g_head.reshape(1, DIFF_V_DIM), q, k_new, v_new, bias, cache_k, cache_v)


def _mlstm_kernel(q_ref, k_ref, v_ref, og_ref, gcol_ref, grow_ref, bcol_ref, brow_ref, gh_ref, c0_ref, n0_ref, m0_ref,
                  h_ref, c_out, n_out, m_out, c_scr, n_scr, m_scr, *, chunk, n_sub, dk, dv, gb):
    j = pl.program_id(1)
    nh = MLSTM_HEADS

    @pl.when(j == 0)
    def _():
        for bb in range(gb):
            for h in range(nh):
                c_scr[bb * nh + h] = c0_ref[bb, h]
                n_scr[bb * nh + h] = n0_ref[bb, h:h + 1, :]
                m_scr[bb * nh + h] = m0_ref[bb, :, h:h + 1]

    r_i = lax.broadcasted_iota(I32, (chunk, chunk), 0)
    c_i = lax.broadcasted_iota(I32, (chunk, chunk), 1)
    causal = c_i <= r_i

    def gates(x, is_input_gate):
        ig = IGATE_SOFTCAP * jnp.tanh(x / IGATE_SOFTCAP)
        lf = jnp.minimum(x, 0.0) - jnp.log1p(jnp.exp(-jnp.abs(x)))
        return jnp.where(is_input_gate, ig, lf)

    def chunk_body(s, carry):
        r0 = pl.multiple_of(s * chunk, chunk)
        for bb in range(gb):
            gc = gcol_ref[bb, s] + bcol_ref[...]
            gr = grow_ref[bb, s] + brow_ref[...]
            gc = gates(gc, lax.broadcasted_iota(I32, gc.shape, 1) < nh)
            gr = gates(gr, lax.broadcasted_iota(I32, gr.shape, 0) < nh)
            for h in range(nh):
                si = bb * nh + h
                i_col, f_col = gc[:, h:h + 1], gc[:, nh + h:nh + h + 1]
                i_row, f_row = gr[h:h + 1, :], gr[nh + h:nh + h + 1, :]
                b_col = jnp.sum(jnp.where(causal, f_row, 0.0), axis=1, keepdims=True)
                b_row = jnp.sum(jnp.where(r_i <= c_i, f_col, 0.0), axis=0, keepdims=True)
                m_prev = m_scr[si]
                c_prev = c_scr[si]
                n_prev = n_scr[si]
                qc = q_ref[bb, pl.ds(r0, chunk), h * dk:(h + 1) * dk] * (dk ** -0.5)
                kc = k_ref[bb, pl.ds(r0, chunk), h * dk:(h + 1) * dk]
                vc = v_ref[bb, pl.ds(r0, chunk), h * dv:(h + 1) * dv].astype(BF16)
                qb16 = qc.astype(BF16)
                dmat = jnp.where(causal, b_col - b_row + i_row, NEG_INF)
                inter = b_col + m_prev
                mt = jnp.maximum(inter, jnp.max(dmat, axis=1, keepdims=True))
                b_end = jnp.sum(f_row, axis=1, keepdims=True)
                d_end = b_end - b_col + i_col
                m_new = jnp.maximum(b_end + m_prev, jnp.max(d_end, axis=0, keepdims=True))
                kw = kc * jnp.exp(d_end - m_new)
                decay = jnp.exp(b_end + m_prev - m_new)
                sm = _dot_nt(qb16, kc.astype(BF16)) * jnp.exp(dmat - mt)
                w_inter = jnp.exp(inter - mt)
                num = _dot(sm.astype(BF16), vc) + w_inter * _dot_nt(qb16, c_prev.astype(BF16))
                den = jnp.sum(sm, axis=1, keepdims=True) + w_inter * jnp.sum(qc * n_prev, axis=1, keepdims=True)
                hh = num / jnp.maximum(jnp.abs(den), jnp.exp(-mt))
                c_scr[si] = decay * c_prev + _dot_tn(vc, kw.astype(BF16))
                n_scr[si] = decay * n_prev + jnp.sum(kw, axis=0, keepdims=True)
                m_scr[si] = m_new
                hn = _rms(hh, gh_ref[:, h * dv:(h + 1) * dv])
                og = og_ref[bb, pl.ds(r0, chunk), h * dv:(h + 1) * dv]
                h_ref[bb, pl.ds(r0, chunk), h * dv:(h + 1) * dv] = hn * (1.0 / (1.0 + jnp.exp(-og)))
        return carry

    lax.fori_loop(0, n_sub, chunk_body, 0)

    @pl.when(j == pl.num_programs(1) - 1)
    def _():
        for bb in range(gb):
            for h in range(nh):
                c_out[bb, h] = c_scr[bb * nh + h]
                n_out[bb, h:h + 1, :] = n_scr[bb * nh + h]
                m_out[bb, :, h:h + 1] = m_scr[bb * nh + h]


def _mlstm(q, k, v, og, gates_raw, b_i, b_f, g_head, c0, n0, m0, tb=512, gb=1):
    b, t, _ = q.shape
    nh = MLSTM_HEADS
    dk = q.shape[2] // nh
    dv = v.shape[2] // nh
    chunk = math.gcd(t, MLSTM_CHUNK)
    tb = min(tb, t)
    gb = math.gcd(gb, b)
    n_sub = tb // chunk
    nc = t // chunk
    g = gates_raw[:, :, :2 * nh].reshape(b, nc, chunk, 2 * nh)
    g_row = jnp.swapaxes(g, 2, 3)
    bias = jnp.concatenate([b_i, b_f]).astype(F32)
    kern = functools.partial(_mlstm_kernel, chunk=chunk, n_sub=n_sub, dk=dk, dv=dv, gb=gb)
    tok = lambda w: pl.BlockSpec((gb, tb, w), lambda bi, j: (bi, j, 0))
    outs = pl.pallas_call(
        kern,
        grid=(b // gb, t // tb),
        in_specs=[
            tok(nh * dk), tok(nh * dk), tok(nh * dv), tok(nh * dv),
            pl.BlockSpec((gb, n_sub, chunk, 2 * nh), lambda bi, j: (bi, j, 0, 0)),
            pl.BlockSpec((gb, n_sub, 2 * nh, chunk), lambda bi, j: (bi, j, 0, 0)),
            pl.BlockSpec((1, 2 * nh), lambda bi, j: (0, 0)),
            pl.BlockSpec((2 * nh, 1), lambda bi, j: (0, 0)),
            pl.BlockSpec((1, nh * dv), lambda bi, j: (0, 0)),
            pl.BlockSpec((gb, nh, dv, dk), lambda bi, j: (bi, 0, 0, 0)),
            pl.BlockSpec((gb, nh, dk), lambda bi, j: (bi, 0, 0)),
            pl.BlockSpec((gb, 1, nh), lambda bi, j: (bi, 0, 0)),
        ],
        out_specs=[
            tok(nh * dv),
            pl.BlockSpec((gb, nh, dv, dk), lambda bi, j: (bi, 0, 0, 0)),
            pl.BlockSpec((gb, nh, dk), lambda bi, j: (bi, 0, 0)),
            pl.BlockSpec((gb, 1, nh), lambda bi, j: (bi, 0, 0)),
        ],
        out_shape=[
            jax.ShapeDtypeStruct((b, t, nh * dv), F32),
            jax.ShapeDtypeStruct((b, nh, dv, dk), F32),
            jax.ShapeDtypeStruct((b, nh, dk), F32),
            jax.ShapeDtypeStruct((b, 1, nh), F32),
        ],
        scratch_shapes=[pltpu.VMEM((gb * nh, dv, dk), F32), pltpu.VMEM((gb * nh, 1, dk), F32),
                        pltpu.VMEM((gb * nh, 1, 1), F32)],
        compiler_params=_cparams(("parallel", "arbitrary")),
    )(q, k, v, og, g, g_row, bias.reshape(1, 2 * nh), bias.reshape(2 * nh, 1), g_head.reshape(1, nh * dv),
      c0, n0, m0.reshape(b, 1, nh))
    hh, c, n, m = outs
    return hh, c, n, m.reshape(b, nh)


_EVEN_SEGS = tuple([(i * HEAD_W, HEAD_W, mode) for i, mode in enumerate(("rope", "rope", "plain", "rope", "rope", "plain"))]
                   + [(6 * HEAD_W, IDX_HEADS * IDX_DIM, "rope"), (6 * HEAD_W + IDX_HEADS * IDX_DIM, LANES, "tail")])


def _pad_cols(w, mult=LANES):
    pad = (-w.shape[1]) % mult
    return jnp.pad(w, ((0, 0), (0, pad))) if pad else w


def _attn_layer(yp, ys, w_in, w_out, lam_rows, g_head, g_mix, lam_init, caches, page_table, li, dims):
    b, t, bd, tn, past = dims
    d = yp.shape[1]
    w_in = _pad_cols(w_in.astype(BF16))
    w_out = w_out.astype(BF16)
    cache_dk, cache_dv, cache_sk, cache_sv, cache_ik = caches

    cos, sin = _rope_tables(jnp.arange(t))
    aq, ak, av, bq, bk, bv, iq, tail, ak_t, bk_t, bv_t, ik_t = _norm_proj(
        yp, g_mix, w_in, _EVEN_SEGS, cos, sin, t_segs=(1, 4, 5, 7))
    r3 = lambda x: x.reshape(b, t, x.shape[-1])
    a_out = _diff_attn_prompt(r3(aq), r3(ak), r3(av), lam_rows, g_head, lam_init)
    b_out = _dsa_prompt(r3(bq), r3(bk), r3(bv), r3(iq), r3(tail))
    mix_p = [a_out.reshape(b * t, HEAD_W), b_out.reshape(b * t, HEAD_W)]
    new_p = (jnp.moveaxis(ak_t.reshape(b, DIFF_HEADS, 2, DIFF_QK_DIM, t), -1, 1),
             av.reshape(b, t, DIFF_HEADS, DIFF_V_DIM),
             jnp.moveaxis(bk_t.reshape(b, DSA_HEADS, DSA_HEAD_DIM, t), -1, 1),
             jnp.moveaxis(bv_t.reshape(b, DSA_HEADS, DSA_HEAD_DIM, t), -1, 1),
             jnp.moveaxis(ik_t, -1, 1))

    cos, sin = _rope_tables(jnp.tile(past + jnp.arange(tn), bd))
    aq, ak, av, bq, bk, bv, iq, tail = _norm_proj(ys, g_mix, w_in, _EVEN_SEGS, cos, sin)
    s3 = lambda x: x.reshape(bd, tn, x.shape[-1])
    npool, page = cache_dk.shape[1], cache_dk.shape[2]
    slot_last = lambda c: jnp.moveaxis(c, 2, -1).reshape(c.shape[0], npool, -1, page)
    bias = _dsa_sample_select(s3(iq), s3(tail), slot_last(cache_ik), page_table, li)
    dv_rows = cache_dv.reshape(cache_dv.shape[0], npool, page * DIFF_HEADS, DIFF_V_DIM)
    a_out = _paged_attn(s3(aq), s3(ak), s3(av), bias, slot_last(cache_dk), dv_rows, page_table, li,
                        lam_rows, g_head, lam_init, diff=True)
    b_out = _paged_attn(s3(bq), s3(bk), s3(bv), bias, slot_last(cache_sk), slot_last(cache_sv), page_table, li,
                        lam_rows, g_head, lam_init, diff=False)
    mix_s = [a_out.reshape(bd * tn, HEAD_W), b_out.reshape(bd * tn, HEAD_W)]
    new_s = (ak.reshape(bd, tn, DIFF_HEADS, 2, DIFF_QK_DIM), av.reshape(bd, tn, DIFF_HEADS, DIFF_V_DIM),
             bk.reshape(bd, tn, DSA_HEADS, DSA_HEAD_DIM), bv.reshape(bd, tn, DSA_HEADS, DSA_HEAD_DIM),
             tail.reshape(bd, tn, LANES)[:, :, :IDX_DIM])
    return mix_p, mix_s, w_out, new_p, new_s


def _mlstm_layer(y, bsz, t, w_in, b_i, b_f, g_head, g_mix, c0, n0, m0):
    d = y.shape[1]
    nh = MLSTM_HEADS
    dk = d // (2 * nh)
    dv = d // nh
    segs = ((0, nh * dk, "plain"), (nh * dk, nh * dk, "plain"), (2 * nh * dk, nh * dv, "plain"),
            (2 * nh * dk + nh * dv, nh * dv, "plain"), (2 * nh * dk + 2 * nh * dv, LANES, "plain"))
    q, k, v, og, gts = _norm_proj(y, g_mix, w_in, segs)
    r3 = lambda x: x.reshape(bsz, t, x.shape[-1])
    hh, c, n, m = _mlstm(r3(q), r3(k), r3(v), r3(og), r3(gts), b_i, b_f, g_head, c0, n0, m0)
    return [hh.reshape(bsz * t, nh * dv)], (c, n, m)


def kernel(x_prompt, x_sample, cache_diff_k, cache_diff_v, cache_dsa_k, cache_dsa_v, cache_idx_k, state_mlstm_c, state_mlstm_n, state_mlstm_m, page_table, g_mix, g_ffn, g_final, w_attn_in, w_attn_out, lambda_q1, lambda_k1, lambda_q2, lambda_k2, g_diff_head, w_mlstm_in, b_mlstm_i, b_mlstm_f, g_mlstm_head, w_mlstm_out, w_ffn_up, w_ffn_down):
    b, t, d = x_prompt.shape
    bd, tn, _ = x_sample.shape
    depth = g_mix.shape[0]
    past = page_table.shape[1] * cache_idx_k.shape[2]
    nh = MLSTM_HEADS
    yp = x_prompt.reshape(b * t, d)
    ys = x_sample.reshape(bd * tn, d)
    new_p, new_s, st_p, st_s = [], [], [], []
    for layer in range(depth):
        li = layer // 2
        last = layer == depth - 1
        if layer % 2 == 0:
            lam_init = 0.8 - 0.6 * math.exp(-0.3 * layer)
            lam_rows = jnp.stack([lambda_q1[li], lambda_k1[li], lambda_q2[li], lambda_k2[li]]).astype(F32)
            mix_p, mix_s, w_out, np_, ns_ = _attn_layer(
                yp, ys, w_attn_in[li], w_attn_out[li], lam_rows, g_diff_head[li], g_mix[layer], lam_init,
                (cache_diff_k, cache_diff_v, cache_dsa_k, cache_dsa_v, cache_idx_k), page_table, li,
                (b, t, bd, tn, past))
            new_p.append(np_)
            new_s.append(ns_)
        else:
            w_in = _pad_cols(w_mlstm_in[li].astype(BF16))
            w_out = w_mlstm_out[li].astype(BF16)
            dk = d // (2 * nh)
            dv = d // nh
            zc = jnp.zeros((b, nh, dv, dk), F32)
            zn = jnp.zeros((b, nh, dk), F32)
            zm = jnp.zeros((b, nh), F32)
            mix_p, sp = _mlstm_layer(yp, b, t, w_in, b_mlstm_i[li], b_mlstm_f[li], g_mlstm_head[li],
                                     g_mix[layer], zc, zn, zm)
            mix_s, ss = _mlstm_layer(ys, bd, tn, w_in, b_mlstm_i[li], b_mlstm_f[li], g_mlstm_head[li],
                                     g_mix[layer], state_mlstm_c[li], state_mlstm_n[li], state_mlstm_m[li])
            st_p.append(sp)
            st_s.append(ss)
        w_up = w_ffn_up[layer].astype(BF16)
        w_down = w_ffn_down[layer].astype(BF16)
        gf = g_final if last else None
        yp = _mix_ffn(mix_p, w_out, yp, g_ffn[layer], w_up, w_down, gf)
        ys = _mix_ffn(mix_s, w_out, ys, g_ffn[layer], w_up, w_down, gf)
    stack = lambda items, idx: jnp.stack([it[idx] for it in items])
    return (yp.reshape(b, t, d), ys.reshape(bd, tn, d),
            stack(new_p, 0), stack(new_p, 1), stack(new_p, 2), stack(new_p, 3), stack(new_p, 4),
            stack(st_p, 0), stack(st_p, 1), stack(st_p, 2),
            stack(new_s, 0), stack(new_s, 1), stack(new_s, 2), stack(new_s, 3), stack(new_s, 4),
            stack(st_s, 0), stack(st_s, 1), stack(st_s, 2))
```

```python
import functools
import math

import jax
import jax.numpy as jnp
from jax import lax
from jax.experimental import pallas as pl
from jax.experimental.pallas import tpu as pltpu

F32 = jnp.float32
BF16 = jnp.bfloat16
I32 = jnp.int32

DIFF_HEADS = 4
DIFF_QK_DIM = 64
DIFF_V_DIM = 128
DSA_HEADS = 8
DSA_HEAD_DIM = 64
IDX_HEADS = 4
IDX_DIM = 64
DSA_TOPK = 256
MLSTM_HEADS = 4
MLSTM_CHUNK = 64
IGATE_SOFTCAP = 15.0
ROPE_THETA = 10000.0
RMS_EPS = 1e-6
HEAD_W = DIFF_HEADS * 2 * DIFF_QK_DIM

LANES = 128
SUBLANES = 8
VMEM_LIMIT = 56 * 1024 * 1024

INT_MIN = -(2 ** 31)
NEG_INF = float("-inf")


def _cparams(sem):
    return pltpu.CompilerParams(dimension_semantics=sem, vmem_limit_bytes=VMEM_LIMIT)


def _rms(x, g):
    return x * lax.rsqrt(jnp.mean(x * x, axis=-1, keepdims=True) + RMS_EPS) * g


def _dot(a, b):
    return jnp.dot(a, b, preferred_element_type=F32)


def _dot_nt(a, b):
    return lax.dot_general(a, b, (((1,), (1,)), ((), ())), preferred_element_type=F32)


def _dot_tn(a, b):
    return lax.dot_general(a, b, (((0,), (0,)), ((), ())), preferred_element_type=F32)


def _norm_proj_kernel(x_ref, g_ref, w_ref, cos_ref, sin_ref, *out_refs, segs, t_segs, tm):
    x = x_ref[...]
    h = _rms(x, g_ref[...]).astype(BF16)
    lane = lax.broadcasted_iota(I32, (tm, LANES), 1)
    first_half = (lane & (IDX_DIM // 2)) == 0
    t_refs = dict(zip(t_segs, out_refs[len(segs):]))
    for si, ((c0, width, mode), o_ref) in enumerate(zip(segs, out_refs)):
        y = _dot(h, w_ref[:, c0:c0 + width])
        for c in range(0, width, LANES):
            r = y[:, c:c + LANES]
            if mode != "plain":
                partner = jnp.where(first_half, pltpu.roll(r, LANES - 32, 1), pltpu.roll(r, 32, 1))
                rot = r * cos_ref[...] + partner * sin_ref[...]
                if mode == "tail":
                    rot = jnp.where(lane < IDX_DIM, rot, r * (IDX_HEADS ** -0.5))
                r = rot
            o_ref[:, c:c + LANES] = r
            if si in t_refs:
                rt = r.T
                if mode == "tail":
                    t_refs[si][0] = rt[:IDX_DIM]
                else:
                    t_refs[si][0, c:c + LANES, :] = rt


def _norm_proj(x, g, w, segs, cos=None, sin=None, t_segs=(), tm=512):
    n, d = x.shape
    tm = min(tm, n)
    assert n % tm == 0
    if cos is None:
        cos = jnp.zeros((tm, LANES), F32)
        sin = cos
    p = cos.shape[0]
    assert p % tm == 0 and n % p == 0
    nper = p // tm
    kern = functools.partial(_norm_proj_kernel, segs=tuple(segs), t_segs=tuple(t_segs), tm=tm)
    t_width = lambda si: IDX_DIM if segs[si][2] == "tail" else segs[si][1]
    return pl.pallas_call(
        kern,
        grid=(n // tm,),
        in_specs=[
            pl.BlockSpec((tm, d), lambda i: (i, 0)),
            pl.BlockSpec((1, d), lambda i: (0, 0)),
            pl.BlockSpec(w.shape, lambda i: (0, 0)),
            pl.BlockSpec((tm, LANES), lambda i: (i % nper, 0)),
            pl.BlockSpec((tm, LANES), lambda i: (i % nper, 0)),
        ],
        out_specs=[pl.BlockSpec((tm, wd), lambda i: (i, 0)) for (_, wd, _) in segs]
        + [pl.BlockSpec((1, t_width(si), tm), lambda i: (i // nper, 0, i % nper)) for si in t_segs],
        out_shape=[jax.ShapeDtypeStruct((n, wd), F32) for (_, wd, _) in segs]
        + [jax.ShapeDtypeStruct((n // p, t_width(si), p), F32) for si in t_segs],
        compiler_params=_cparams(("parallel",)),
    )(x, g.reshape(1, d), w, cos, sin)


def _rope_tables(pos):
    half = IDX_DIM // 2
    inv = ROPE_THETA ** (-jnp.arange(half, dtype=F32) / half)
    ang = pos.astype(F32)[:, None] * inv[None, :]
    cos = jnp.cos(ang)
    sin = jnp.sin(ang)
    cos_t = jnp.tile(cos, (1, LANES // half))
    sin_t = jnp.tile(jnp.concatenate([-sin, sin], axis=1), (1, LANES // IDX_DIM))
    return cos_t, sin_t


def _mix_ffn_kernel(*refs, widths, final_norm):
    n_in = len(widths)
    x_refs = refs[:n_in]
    wo_ref, res_ref, g_ref, wu_ref, wd_ref, gf_ref, o_ref, h_scr, acc_scr = refs[n_in:]
    f = pl.program_id(1)

    @pl.when(f == 0)
    def _():
        y = res_ref[...]
        r0 = 0
        for x_ref, wd in zip(x_refs, widths):
            y = y + _dot(x_ref[...].astype(BF16), wo_ref[r0:r0 + wd, :])
            r0 += wd
        h_scr[...] = _rms(y, g_ref[...]).astype(BF16)
        acc_scr[...] = y

    u = jnp.maximum(_dot(h_scr[...], wu_ref[...]), 0.0)
    acc_scr[...] += _dot((u * u).astype(BF16), wd_ref[...])

    @pl.when(f == pl.num_programs(1) - 1)
    def _():
        y = acc_scr[...]
        if final_norm:
            y = _rms(y, gf_ref[...])
        o_ref[...] = y


def _mix_ffn(xs, w_out, res, g, w_up, w_down, g_final=None, tm=512, tf=2048):
    n, d = res.shape
    dff = w_up.shape[1]
    tm = min(tm, n)
    widths = tuple(x.shape[1] for x in xs)
    final_norm = g_final is not None
    if g_final is None:
        g_final = g
    kern = functools.partial(_mix_ffn_kernel, widths=widths, final_norm=final_norm)
    row = lambda wd: pl.BlockSpec((tm, wd), lambda i, f: (i, 0))
    vec = pl.BlockSpec((1, d), lambda i, f: (0, 0))
    return pl.pallas_call(
        kern,
        grid=(n // tm, dff // tf),
        in_specs=[row(wd) for wd in widths] + [
            pl.BlockSpec(w_out.shape, lambda i, f: (0, 0)),
            row(d),
            vec,
            pl.BlockSpec((d, tf), lambda i, f: (0, f)),
            pl.BlockSpec((tf, d), lambda i, f: (f, 0)),
            vec,
        ],
        out_specs=row(d),
        out_shape=jax.ShapeDtypeStruct((n, d), F32),
        scratch_shapes=[pltpu.VMEM((tm, d), BF16), pltpu.VMEM((tm, d), F32)],
        compiler_params=_cparams(("parallel", "arbitrary")),
    )(*xs, w_out, res, g.reshape(1, d), w_up, w_down, g_final.reshape(1, d))


def _lambda(lam_ref, lam_init):
    l = lam_ref[...]
    a = jnp.sum(l[0:1] * l[1:2], axis=1, keepdims=True)
    b = jnp.sum(l[2:3] * l[3:4], axis=1, keepdims=True)
    return jnp.exp(a) - jnp.exp(b) + lam_init


def _lane_tile(x, n):
    return x if n == 1 else jnp.concatenate([x] * n, axis=1)


def _split_halves(q):
    lane = lax.broadcasted_iota(I32, q.shape, 1)
    lo = lane < DSA_HEAD_DIM
    return jnp.concatenate([jnp.where(lo, q, 0.0), jnp.where(lo, 0.0, q)], axis=0)


def _key_to_float(key):
    return pltpu.bitcast(jnp.where(key >= 0, key, key ^ jnp.int32(0x7FFFFFFF)), F32)


def _kth_largest(count_ge, rows, kk):
    def body(it, t):
        cand = t + jnp.left_shift(jnp.int32(1), 31 - it)
        return jnp.where(count_ge(_key_to_float(cand)) >= kk, cand, t)

    return _key_to_float(lax.fori_loop(0, 32, body, jnp.full((rows, 1), INT_MIN, I32)))


def _diff_attn_kernel(lam_ref, g_ref, q_ref, k_ref, v_ref, o_ref, m_scr, l_scr, acc_scr, *, blk, lam_init):
    i = pl.program_id(2)
    q2 = _split_halves(q_ref[0] * (DIFF_QK_DIM ** -0.5)).astype(BF16)
    m_scr[...] = jnp.full(m_scr.shape, NEG_INF, F32)
    l_scr[...] = jnp.zeros(l_scr.shape, F32)
    acc_scr[...] = jnp.zeros(acc_scr.shape, F32)

    def step(j, on_diagonal):
        start = pl.multiple_of(j * blk, blk)
        kj = k_ref[0, pl.ds(start, blk), :].astype(BF16)
        vj = v_ref[0, pl.ds(start, blk), :].astype(BF16)
        s = _dot_nt(q2, kj)
        if on_diagonal:
            row = lax.broadcasted_iota(I32, (2 * blk, blk), 0)
            col = lax.broadcasted_iota(I32, (2 * blk, blk), 1)
            s = jnp.where(col <= jnp.where(row >= blk, row - blk, row), s, NEG_INF)
        m_old = m_scr[...]
        m_new = jnp.maximum(m_old, jnp.max(s, axis=1, keepdims=True))
        alpha = jnp.exp(m_old - m_new)
        p = jnp.exp(s - _lane_tile(m_new, blk // LANES))
        l_scr[...] = alpha * l_scr[...] + jnp.sum(p, axis=1, keepdims=True)
        acc_scr[...] = alpha * acc_scr[...] + _dot(p.astype(BF16), vj)
        m_scr[...] = m_new

    def body(j, carry):
        step(j, False)
        return carry

    lax.fori_loop(0, i, body, 0)
    step(i, True)
    o = acc_scr[...] / l_scr[...]
    a = o[:blk] - _lambda(lam_ref, lam_init) * o[blk:]
    o_ref[0] = _rms(a, g_ref[...]) * (1.0 - lam_init)


def _diff_attn_prompt(q, k, v, lam_rows, g_head, lam_init, blk=512):
    b, t, _ = q.shape
    blk = min(blk, t)
    kern = functools.partial(_diff_attn_kernel, blk=blk, lam_init=lam_init)
    return pl.pallas_call(
        kern,
        grid=(b, DIFF_HEADS, t // blk),
        in_specs=[
            pl.BlockSpec(lam_rows.shape, lambda bi, h, i: (0, 0)),
            pl.BlockSpec((1, DIFF_V_DIM), lambda bi, h, i: (0, 0)),
            pl.BlockSpec((1, blk, LANES), lambda bi, h, i: (bi, i, h)),
            pl.BlockSpec((1, t, LANES), lambda bi, h, i: (bi, 0, h)),
            pl.BlockSpec((1, t, LANES), lambda bi, h, i: (bi, 0, h)),
        ],
        out_specs=pl.BlockSpec((1, blk, LANES), lambda bi, h, i: (bi, i, h)),
        out_shape=jax.ShapeDtypeStruct((b, t, HEAD_W), F32),
        scratch_shapes=[pltpu.VMEM((2 * blk, LANES), F32), pltpu.VMEM((2 * blk, LANES), F32),
                        pltpu.VMEM((2 * blk, LANES), F32)],
        compiler_params=_cparams(("parallel", "parallel", "parallel")),
    )(lam_rows, g_head.reshape(1, DIFF_V_DIM), q, k, v)


def _dsa_prompt_kernel(q_ref, iq_ref, tail_q_ref, k_ref, v_ref, tail_k_ref, o_ref, key_scr, bias_scr,
                       *, qb, t, topk, wstep):
    i = pl.program_id(1)

    def body(w):
        qpos = i * qb + lax.broadcasted_iota(I32, (qb, w), 0)
        kpos = lax.broadcasted_iota(I32, (qb, w), 1)
        causal = kpos <= qpos

        ik = tail_k_ref[0, :w, :IDX_DIM].astype(BF16)
        tq = tail_q_ref[0]
        score = jnp.zeros((qb, w), F32)
        for h in range(IDX_HEADS):
            qh = (iq_ref[0, :, h * IDX_DIM:(h + 1) * IDX_DIM] * (IDX_DIM ** -0.5)).astype(BF16)
            score = score + jnp.maximum(_dot_nt(qh, ik), 0.0) * tq[:, IDX_DIM + h:IDX_DIM + h + 1]
        key_scr[:, :w] = jnp.where(causal, score, NEG_INF)

        def count_ge(cand):
            return jnp.sum(jnp.where(key_scr[:, :w] >= cand, 1.0, 0.0), axis=1, keepdims=True)

        thr = _kth_largest(count_ge, qb, float(topk))
        key = key_scr[:, :w]
        above = (key > thr) | (qpos < topk)
        n_above = jnp.sum(jnp.where(above, 1.0, 0.0), axis=1, keepdims=True)
        room = float(topk) - n_above
        tie = key == thr
        tri = (lax.broadcasted_iota(I32, (LANES, LANES), 0) <= lax.broadcasted_iota(I32, (LANES, LANES), 1))
        tri = jnp.where(tri, 1.0, 0.0).astype(BF16)
        before = jnp.zeros((qb, 1), F32)
        for c in range(0, w, LANES):
            tc = tie[:, c:c + LANES]
            rank = _dot(jnp.where(tc, 1.0, 0.0).astype(BF16), tri) + before
            sel = (above[:, c:c + LANES] | (tc & (rank <= room))) & causal[:, c:c + LANES]
            bias_scr[:, c:c + LANES] = jnp.where(sel, 0.0, NEG_INF)
            before = rank[:, LANES - 1:LANES]

        lane = lax.broadcasted_iota(I32, (qb, LANES), 1)
        for pb in range(DSA_HEADS // 2):
            cols = slice(pb * LANES, (pb + 1) * LANES)
            q2 = _split_halves(q_ref[0, :, cols] * (DSA_HEAD_DIM ** -0.5)).astype(BF16)
            kp = k_ref[0, :w, cols].astype(BF16)
            vp = v_ref[0, :w, cols].astype(BF16)
            outs = []
            for m in range(2):
                s = _dot_nt(q2[m * qb:(m + 1) * qb], kp) + bias_scr[:, :w]
                mx = jnp.max(s, axis=1, keepdims=True)
                p = jnp.exp(s - mx)
                den = jnp.sum(p, axis=1, keepdims=True)
                outs.append(_dot(p.astype(BF16), vp) / den)
            o_ref[0, :, cols] = jnp.where(lane < DSA_HEAD_DIM, outs[0], outs[1])

    n_w = t // wstep
    needed = ((i + 1) * qb + wstep - 1) // wstep
    for a in range(1, n_w + 1):
        pl.when(needed == a)(functools.partial(body, a * wstep))


def _dsa_prompt(q, k, v, iq, tail, qb=256, wstep=256):
    b, t, _ = q.shape
    qb = min(qb, t)
    wstep = min(wstep, t)
    assert t % wstep == 0 and wstep % qb == 0
    topk = min(DSA_TOPK, t // 4)
    kern = functools.partial(_dsa_prompt_kernel, qb=qb, t=t, topk=topk, wstep=wstep)
    return pl.pallas_call(
        kern,
        grid=(b, t // qb),
        in_specs=[
            pl.BlockSpec((1, qb, HEAD_W), lambda bi, i: (bi, i, 0)),
            pl.BlockSpec((1, qb, IDX_HEADS * IDX_DIM), lambda bi, i: (bi, i, 0)),
            pl.BlockSpec((1, qb, LANES), lambda bi, i: (bi, i, 0)),
            pl.BlockSpec((1, t, HEAD_W), lambda bi, i: (bi, 0, 0)),
            pl.BlockSpec((1, t, HEAD_W), lambda bi, i: (bi, 0, 0)),
            pl.BlockSpec((1, t, LANES), lambda bi, i: (bi, 0, 0)),
        ],
        out_specs=pl.BlockSpec((1, qb, HEAD_W), lambda bi, i: (bi, i, 0)),
        out_shape=jax.ShapeDtypeStruct((b, t, HEAD_W), F32),
        scratch_shapes=[pltpu.VMEM((qb, t), F32), pltpu.VMEM((qb, t), F32)],
        compiler_params=_cparams(("parallel", "parallel")),
    )(q, iq, tail, k, v, tail)


def _fetch_pages(pt_ref, li, pg, streams, sem):
    b = pl.program_id(0)
    j = pl.program_id(1)
    n_j = pl.num_programs(1)
    step = b * n_j + j
    slot = step % 2

    def page_copies(bb, jj, sl):
        cps = []
        for g in range(pg):
            pid = pt_ref[bb, jj * pg + g]
            for k, (src, buf) in enumerate(streams):
                cps.append(pltpu.make_async_copy(src.at[li, pid], buf.at[sl, g], sem.at[sl, k]))
        return cps

    def start_all(cps):
        for n, cp in enumerate(cps):
            cp.start(priority=n % 2)

    @pl.when(step == 0)
    def _():
        start_all(page_copies(b, j, slot))

    @pl.when(step + 1 < pl.num_programs(0) * n_j)
    def _():
        wrap = j + 1 == n_j
        start_all(page_copies(jnp.where(wrap, b + 1, b), jnp.where(wrap, 0, j + 1), 1 - slot))

    for cp in page_copies(b, j, slot):
        cp.wait()
    return slot


def _dsa_sample_select_kernel(pt_ref, iq_ref, tail_ref, cik_ref, bias_ref, pbuf, sem, key_scr,
                              *, pg, tn, n_chunks, topk, li):
    j = pl.program_id(1)
    slot = _fetch_pages(pt_ref, li, pg, [(cik_ref, pbuf)], sem)
    tail = tail_ref[0]
    iq = iq_ref[0]
    q_rows = jnp.concatenate([iq[:, h * IDX_DIM:(h + 1) * IDX_DIM] for h in range(IDX_HEADS)], axis=0)
    q_rows = (q_rows * (IDX_DIM ** -0.5)).astype(BF16)
    w_rows = jnp.concatenate([tail[:, IDX_DIM + h:IDX_DIM + h + 1] for h in range(IDX_HEADS)], axis=0)

    def scores(dots):
        d = jnp.maximum(dots, 0.0) * w_rows
        s = d[0:tn]
        for h in range(1, IDX_HEADS):
            s = s + d[h * tn:(h + 1) * tn]
        return s

    k_step = jnp.concatenate([pbuf[slot, g] for g in range(pg)], axis=1).astype(BF16)
    s_step = scores(_dot(q_rows, k_step))
    for g in range(pg):
        key_scr[j * pg + g] = s_step[:, g * LANES:(g + 1) * LANES]

    @pl.when(j == pl.num_programs(1) - 1)
    def _():
        n_past = n_chunks - 1
        k_new = jnp.concatenate([tail[:, :IDX_DIM], jnp.zeros((LANES - tn, IDX_DIM), F32)], axis=0).astype(BF16)
        s_new = scores(_dot_nt(q_rows, k_new))
        row = lax.broadcasted_iota(I32, (tn, LANES), 0)
        col = lax.broadcasted_iota(I32, (tn, LANES), 1)
        key_scr[n_past] = jnp.where(col <= row, s_new, NEG_INF)

        def count_ge(cand):
            n_part = min(16, n_past)
            per = n_past // n_part
            tot = jnp.where(key_scr[n_past] >= cand, 1.0, 0.0)
            for a in range(n_part):
                hi = n_past if a == n_part - 1 else (a + 1) * per
                tot = tot + jnp.sum(jnp.where(key_scr[a * per:hi] >= cand[None], 1.0, 0.0), axis=0)
            return jnp.sum(tot, axis=1, keepdims=True)

        thr = _kth_largest(count_ge, tn, float(topk))
        key = key_scr[...]
        n_valid = n_past * LANES + 1 + lax.broadcasted_iota(I32, (tn, 1), 0)
        above = (key > thr[None]) | (n_valid <= topk)[None]
        n_above = jnp.sum(jnp.sum(jnp.where(above, 1.0, 0.0), axis=0), axis=1, keepdims=True)
        room = float(topk) - n_above
        tie = jnp.where(key == thr[None], 1.0, 0.0)
        tri = (lax.broadcasted_iota(I32, (LANES, LANES), 0) <= lax.broadcasted_iota(I32, (LANES, LANES), 1))
        tri = jnp.where(tri, 1.0, 0.0).astype(BF16)
        within = _dot(tie.reshape(n_chunks * tn, LANES).astype(BF16), tri).reshape(n_chunks, tn, LANES)
        totals = within[:, :, LANES - 1:LANES]

        def chunk_body(c, before):
            rank = within[c] + before
            sel = above[c] | ((tie[c] > 0) & (rank <= room))
            bias_ref[0, c] = jnp.where(sel, 0.0, NEG_INF)
            return before + totals[c]

        before = jnp.zeros((tn, 1), F32)
        for c in range(n_chunks):
            before = chunk_body(c, before)
        bias_ref[0, n_past] = jnp.where(col <= row, bias_ref[0, n_past], NEG_INF)


def _dsa_sample_select(iq, tail, cache_ik_t, page_table, li, pg=32):
    bd, tn, _ = iq.shape
    n_pages = page_table.shape[1]
    page = cache_ik_t.shape[3]
    pg = min(pg, n_pages)
    assert page == LANES and n_pages % pg == 0
    n_chunks = n_pages + 1
    topk = min(DSA_TOPK, (n_pages * page + tn) // 4)
    kern = functools.partial(_dsa_sample_select_kernel, pg=pg, tn=tn, n_chunks=n_chunks, topk=topk, li=li)
    return pl.pallas_call(
        kern,
        grid_spec=pltpu.PrefetchScalarGridSpec(
            num_scalar_prefetch=1,
            grid=(bd, n_pages // pg),
            in_specs=[
                pl.BlockSpec((1, tn, IDX_HEADS * IDX_DIM), lambda b, j, pt: (b, 0, 0)),
                pl.BlockSpec((1, tn, LANES), lambda b, j, pt: (b, 0, 0)),
                pl.BlockSpec(memory_space=pl.ANY),
            ],
            out_specs=pl.BlockSpec((1, n_chunks, tn, LANES), lambda b, j, pt: (b, 0, 0, 0)),
            scratch_shapes=[pltpu.VMEM((2, pg, IDX_DIM, page), F32), pltpu.SemaphoreType.DMA((2, 1)),
                            pltpu.VMEM((n_chunks, tn, LANES), F32)],
        ),
        out_shape=jax.ShapeDtypeStruct((bd, n_chunks, tn, LANES), F32),
        compiler_params=_cparams(("arbitrary", "arbitrary")),
    )(page_table, iq, tail, cache_ik_t)


def _paged_attn_kernel(pt_ref, lam_ref, g_ref, q_ref, kn_ref, vn_ref, bias_ref, ck_ref, cv_ref, o_ref,
                       kbuf, vbuf, sem, m_scr, l_scr, acc_scr, *, pg, tn, diff, lam_init, li):
    j = pl.program_id(1)
    nblk = HEAD_W // LANES
    rows = 2 * tn
    scale = DIFF_QK_DIM ** -0.5
    slot = _fetch_pages(pt_ref, li, pg, [(ck_ref, kbuf), (cv_ref, vbuf)], sem)

    def update(blk, s_parts, pv_fn):
        s = jnp.concatenate(s_parts, axis=1)
        m_old = m_scr[blk]
        m_new = jnp.maximum(m_old, jnp.max(s, axis=1, keepdims=True))
        alpha = jnp.exp(m_old - m_new)
        p = jnp.exp(s - _lane_tile(m_new, len(s_parts)))
        l_scr[blk] = alpha * l_scr[blk] + jnp.sum(p, axis=1, keepdims=True)
        pv = None
        for g in range(len(s_parts)):
            term = pv_fn(g, p[:, g * LANES:(g + 1) * LANES].astype(BF16))
            pv = term if pv is None else pv + term
        acc_scr[blk] = alpha * acc_scr[blk] + pv
        m_scr[blk] = m_new

    q_all = q_ref[0] * scale
    q2 = [_split_halves(q_all[:, b * LANES:(b + 1) * LANES]).astype(BF16) for b in range(nblk)]

    @pl.when(j == 0)
    def _():
        m_scr[...] = jnp.full(m_scr.shape, jnp.finfo(F32).min, F32)
        l_scr[...] = jnp.zeros(l_scr.shape, F32)
        acc_scr[...] = jnp.zeros(acc_scr.shape, F32)
        n_past = bias_ref.shape[1] - 1
        row = lax.broadcasted_iota(I32, (rows, LANES), 0)
        col = lax.broadcasted_iota(I32, (rows, LANES), 1)
        causal = col <= jnp.where(row >= tn, row - tn, row)
        bn = bias_ref[0, n_past]
        bn2 = jnp.concatenate([bn, bn], axis=0)
        pad = jnp.zeros((LANES - tn, LANES), F32)
        for blk in range(nblk):
            cols = slice(blk * LANES, (blk + 1) * LANES)
            kn = jnp.concatenate([kn_ref[0, :, cols], pad], axis=0).astype(BF16)
            vn = jnp.concatenate([vn_ref[0, :, cols], pad], axis=0).astype(BF16)
            s = jnp.where(causal, _dot_nt(q2[blk], kn), NEG_INF)
            if not diff:
                s = s + bn2
            update(blk, [s], lambda g, p: _dot(p, vn))

    if not diff:
        bias2 = []
        for g in range(pg):
            bp = bias_ref[0, j * pg + g]
            bias2.append(jnp.concatenate([bp, bp], axis=0))
    for blk in range(nblk):
        rows_blk = slice(blk * LANES, (blk + 1) * LANES)
        s_parts = [_dot(q2[blk], kbuf[slot, g, rows_blk, :].astype(BF16)) for g in range(pg)]
        if diff:
            update(blk, s_parts,
                   lambda g, p: _dot(p, vbuf[slot, g, pl.ds(blk, LANES, stride=nblk), :].astype(BF16)))
        else:
            update(blk, [s + b2 for s, b2 in zip(s_parts, bias2)],
                   lambda g, p: _dot_nt(p, vbuf[slot, g, rows_blk, :].astype(BF16)))

    @pl.when(j == pl.num_programs(1) - 1)
    def _():
        lane = lax.broadcasted_iota(I32, (tn, LANES), 1)
        for blk in range(nblk):
            o = acc_scr[blk] / l_scr[blk]
            if diff:
                a = o[:tn] - _lambda(lam_ref, lam_init) * o[tn:]
                r = _rms(a, g_ref[...]) * (1.0 - lam_init)
            else:
                r = jnp.where(lane < DSA_HEAD_DIM, o[:tn], o[tn:])
            o_ref[0, :, blk * LANES:(blk + 1) * LANES] = r


def _paged_attn(q, k_new, v_new, bias, cache_k, cache_v, page_table, li, lam_rows, g_head, lam_init, diff, pg=16):
    bd, tn, _ = q.shape
    n_pages = page_table.shape[1]
    pg = min(pg, n_pages)
    assert n_pages % pg == 0 and cache_k.shape[2:] == (HEAD_W, LANES) and cache_v.shape[2:] == (HEAD_W, LANES)
    n_chunks = n_pages + 1
    kern = functools.partial(_paged_attn_kernel, pg=pg, tn=tn, diff=diff, lam_init=lam_init, li=li)
    nblk = HEAD_W // LANES
    tok = pl.BlockSpec((1, tn, HEAD_W), lambda b, j, pt: (b, 0, 0))
    stats = pltpu.VMEM((nblk, 2 * tn, LANES), F32)
    pages = pltpu.VMEM((2, pg, HEAD_W, LANES), F32)
    return pl.pallas_call(
        kern,
        grid_spec=pltpu.PrefetchScalarGridSpec(
            num_scalar_prefetch=1,
            grid=(bd, n_pages // pg),
            in_specs=[
                pl.BlockSpec(lam_rows.shape, lambda b, j, pt: (0, 0)),
                pl.BlockSpec((1, DIFF_V_DIM), lambda b, j, pt: (0, 0)),
                tok, tok, tok,
                pl.BlockSpec((1, n_chunks, tn, LANES), lambda b, j, pt: (b, 0, 0, 0)),
                pl.BlockSpec(memory_space=pl.ANY),
                pl.BlockSpec(memory_space=pl.ANY),
            ],
            out_specs=pl.BlockSpec((1, tn, HEAD_W), lambda b, j, pt: (b, 0, 0)),
            scratch_shapes=[pages, pages, pltpu.SemaphoreType.DMA((2, 2)), stats, stats, stats],
        ),
        out_shape=jax.ShapeDtypeStruct((bd, tn, HEAD_W), F32),
        compiler_params=_cparams(("arbitrary", "arbitrary")),
    )(page_table, lam_rows, g_head.reshape(1, DIFF_V_DIM), q, k_new, v_new, bias, cache_k, cache_v)


def _mlstm_kernel(q_ref, k_ref, v_ref, og_ref, gcol_ref, grow_ref, bcol_ref, brow_ref, gh_ref, c0_ref, n0_ref, m0_ref,
                  h_ref, c_out, n_out, m_out, c_scr, n_scr, m_scr, *, chunk, n_sub, dk, dv, gb):
    j = pl.program_id(1)
    nh = MLSTM_HEADS

    @pl.when(j == 0)
    def _():
        for bb in range(gb):
            for h in range(nh):
                c_scr[bb * nh + h] = c0_ref[bb, h]
                n_scr[bb * nh + h] = n0_ref[bb, h:h + 1, :]
                m_scr[bb * nh + h] = m0_ref[bb, :, h:h + 1]

    r_i = lax.broadcasted_iota(I32, (chunk, chunk), 0)
    c_i = lax.broadcasted_iota(I32, (chunk, chunk), 1)
    causal = c_i <= r_i

    def gates(x, is_input_gate):
        ig = IGATE_SOFTCAP * jnp.tanh(x / IGATE_SOFTCAP)
        lf = jnp.minimum(x, 0.0) - jnp.log1p(jnp.exp(-jnp.abs(x)))
        return jnp.where(is_input_gate, ig, lf)

    def chunk_body(s, carry):
        r0 = pl.multiple_of(s * chunk, chunk)
        for bb in range(gb):
            gc = gcol_ref[bb, s] + bcol_ref[...]
            gr = grow_ref[bb, s] + brow_ref[...]
            gc = gates(gc, lax.broadcasted_iota(I32, gc.shape, 1) < nh)
            gr = gates(gr, lax.broadcasted_iota(I32, gr.shape, 0) < nh)
            for h in range(nh):
                si = bb * nh + h
                i_col, f_col = gc[:, h:h + 1], gc[:, nh + h:nh + h + 1]
                i_row, f_row = gr[h:h + 1, :], gr[nh + h:nh + h + 1, :]
                b_col = jnp.sum(jnp.where(causal, f_row, 0.0), axis=1, keepdims=True)
                b_row = jnp.sum(jnp.where(r_i <= c_i, f_col, 0.0), axis=0, keepdims=True)
                m_prev = m_scr[si]
                c_prev = c_scr[si]
                n_prev = n_scr[si]
                qc = q_ref[bb, pl.ds(r0, chunk), h * dk:(h + 1) * dk] * (dk ** -0.5)
                kc = k_ref[bb, pl.ds(r0, chunk), h * dk:(h + 1) * dk]
                vc = v_ref[bb, pl.ds(r0, chunk), h * dv:(h + 1) * dv].astype(BF16)
                qb16 = qc.astype(BF16)
                dmat = jnp.where(causal, b_col - b_row + i_row, NEG_INF)
                inter = b_col + m_prev
                mt = jnp.maximum(inter, jnp.max(dmat, axis=1, keepdims=True))
                b_end = jnp.sum(f_row, axis=1, keepdims=True)
                d_end = b_end - b_col + i_col
                m_new = jnp.maximum(b_end + m_prev, jnp.max(d_end, axis=0, keepdims=True))
                kw = kc * jnp.exp(d_end - m_new)
                decay = jnp.exp(b_end + m_prev - m_new)
                sm = _dot_nt(qb16, kc.astype(BF16)) * jnp.exp(dmat - mt)
                w_inter = jnp.exp(inter - mt)
                num = _dot(sm.astype(BF16), vc) + w_inter * _dot_nt(qb16, c_prev.astype(BF16))
                den = jnp.sum(sm, axis=1, keepdims=True) + w_inter * jnp.sum(qc * n_prev, axis=1, keepdims=True)
                hh = num / jnp.maximum(jnp.abs(den), jnp.exp(-mt))
                c_scr[si] = decay * c_prev + _dot_tn(vc, kw.astype(BF16))
                n_scr[si] = decay * n_prev + jnp.sum(kw, axis=0, keepdims=True)
                m_scr[si] = m_new
                hn = _rms(hh, gh_ref[:, h * dv:(h + 1) * dv])
                og = og_ref[bb, pl.ds(r0, chunk), h * dv:(h + 1) * dv]
                h_ref[bb, pl.ds(r0, chunk), h * dv:(h + 1) * dv] = hn * (1.0 / (1.0 + jnp.exp(-og)))
        return carry

    lax.fori_loop(0, n_sub, chunk_body, 0)

    @pl.when(j == pl.num_programs(1) - 1)
    def _():
        for bb in range(gb):
            for h in range(nh):
                c_out[bb, h] = c_scr[bb * nh + h]
                n_out[bb, h:h + 1, :] = n_scr[bb * nh + h]
                m_out[bb, :, h:h + 1] = m_scr[bb * nh + h]


def _mlstm(q, k, v, og, gates_raw, b_i, b_f, g_head, c0, n0, m0, tb=512, gb=1):
    b, t, _ = q.shape
    nh = MLSTM_HEADS
    dk = q.shape[2] // nh
    dv = v.shape[2] // nh
    chunk = math.gcd(t, MLSTM_CHUNK)
    tb = min(tb, t)
    gb = math.gcd(gb, b)
    n_sub = tb // chunk
    nc = t // chunk
    g = gates_raw[:, :, :2 * nh].reshape(b, nc, chunk, 2 * nh)
    g_row = jnp.swapaxes(g, 2, 3)
    bias = jnp.concatenate([b_i, b_f]).astype(F32)
    kern = functools.partial(_mlstm_kernel, chunk=chunk, n_sub=n_sub, dk=dk, dv=dv, gb=gb)
    tok = lambda w: pl.BlockSpec((gb, tb, w), lambda bi, j: (bi, j, 0))
    outs = pl.pallas_call(
        kern,
        grid=(b // gb, t // tb),
        in_specs=[
            tok(nh * dk), tok(nh * dk), tok(nh * dv), tok(nh * dv),
            pl.BlockSpec((gb, n_sub, chunk, 2 * nh), lambda bi, j: (bi, j, 0, 0)),
            pl.BlockSpec((gb, n_sub, 2 * nh, chunk), lambda bi, j: (bi, j, 0, 0)),
            pl.BlockSpec((1, 2 * nh), lambda bi, j: (0, 0)),
            pl.BlockSpec((2 * nh, 1), lambda bi, j: (0, 0)),
            pl.BlockSpec((1, nh * dv), lambda bi, j: (0, 0)),
            pl.BlockSpec((gb, nh, dv, dk), lambda bi, j: (bi, 0, 0, 0)),
            pl.BlockSpec((gb, nh, dk), lambda bi, j: (bi, 0, 0)),
            pl.BlockSpec((gb, 1, nh), lambda bi, j: (bi, 0, 0)),
        ],
        out_specs=[
            tok(nh * dv),
            pl.BlockSpec((gb, nh, dv, dk), lambda bi, j: (bi, 0, 0, 0)),
            pl.BlockSpec((gb, nh, dk), lambda bi, j: (bi, 0, 0)),
            pl.BlockSpec((gb, 1, nh), lambda bi, j: (bi, 0, 0)),
        ],
        out_shape=[
            jax.ShapeDtypeStruct((b, t, nh * dv), F32),
            jax.ShapeDtypeStruct((b, nh, dv, dk), F32),
            jax.ShapeDtypeStruct((b, nh, dk), F32),
            jax.ShapeDtypeStruct((b, 1, nh), F32),
        ],
        scratch_shapes=[pltpu.VMEM((gb * nh, dv, dk), F32), pltpu.VMEM((gb * nh, 1, dk), F32),
                        pltpu.VMEM((gb * nh, 1, 1), F32)],
        compiler_params=_cparams(("parallel", "arbitrary")),
    )(q, k, v, og, g, g_row, bias.reshape(1, 2 * nh), bias.reshape(2 * nh, 1), g_head.reshape(1, nh * dv),
      c0, n0, m0.reshape(b, 1, nh))
    hh, c, n, m = outs
    return hh, c, n, m.reshape(b, nh)


_EVEN_SEGS = tuple([(i * HEAD_W, HEAD_W, mode) for i, mode in enumerate(("rope", "rope", "plain", "rope", "rope", "plain"))]
                   + [(6 * HEAD_W, IDX_HEADS * IDX_DIM, "rope"), (6 * HEAD_W + IDX_HEADS * IDX_DIM, LANES, "tail")])


def _pad_cols(w, mult=LANES):
    pad = (-w.shape[1]) % mult
    return jnp.pad(w, ((0, 0), (0, pad))) if pad else w


def _attn_layer(yp, ys, w_in, w_out, lam_rows, g_head, g_mix, lam_init, caches, page_table, li, dims):
    b, t, bd, tn, past = dims
    d = yp.shape[1]
    w_in = _pad_cols(w_in.astype(BF16))
    w_out = w_out.astype(BF16)
    cache_dk, cache_dv, cache_sk, cache_sv, cache_ik = caches

    cos, sin = _rope_tables(jnp.arange(t))
    aq, ak, av, bq, bk, bv, iq, tail, ak_t, bk_t, bv_t, ik_t = _norm_proj(
        yp, g_mix, w_in, _EVEN_SEGS, cos, sin, t_segs=(1, 4, 5, 7))
    r3 = lambda x: x.reshape(b, t, x.shape[-1])
    a_out = _diff_attn_prompt(r3(aq), r3(ak), r3(av), lam_rows, g_head, lam_init)
    b_out = _dsa_prompt(r3(bq), r3(bk), r3(bv), r3(iq), r3(tail))
    mix_p = [a_out.reshape(b * t, HEAD_W), b_out.reshape(b * t, HEAD_W)]
    new_p = (jnp.moveaxis(ak_t.reshape(b, DIFF_HEADS, 2, DIFF_QK_DIM, t), -1, 1),
             av.reshape(b, t, DIFF_HEADS, DIFF_V_DIM),
             jnp.moveaxis(bk_t.reshape(b, DSA_HEADS, DSA_HEAD_DIM, t), -1, 1),
             jnp.moveaxis(bv_t.reshape(b, DSA_HEADS, DSA_HEAD_DIM, t), -1, 1),
             jnp.moveaxis(ik_t, -1, 1))

    cos, sin = _rope_tables(jnp.tile(past + jnp.arange(tn), bd))
    aq, ak, av, bq, bk, bv, iq, tail = _norm_proj(ys, g_mix, w_in, _EVEN_SEGS, cos, sin)
    s3 = lambda x: x.reshape(bd, tn, x.shape[-1])
    npool, page = cache_dk.shape[1], cache_dk.shape[2]
    slot_last = lambda c: jnp.moveaxis(c, 2, -1).reshape(c.shape[0], npool, -1, page)
    bias = _dsa_sample_select(s3(iq), s3(tail), slot_last(cache_ik), page_table, li)
    dv_rows = cache_dv.reshape(cache_dv.shape[0], npool, page * DIFF_HEADS, DIFF_V_DIM)
    a_out = _paged_attn(s3(aq), s3(ak), s3(av), bias, slot_last(cache_dk), dv_rows, page_table, li,
                        lam_rows, g_head, lam_init, diff=True)
    b_out = _paged_attn(s3(bq), s3(bk), s3(bv), bias, slot_last(cache_sk), slot_last(cache_sv), page_table, li,
                        lam_rows, g_head, lam_init, diff=False)
    mix_s = [a_out.reshape(bd * tn, HEAD_W), b_out.reshape(bd * tn, HEAD_W)]
    new_s = (ak.reshape(bd, tn, DIFF_HEADS, 2, DIFF_QK_DIM), av.reshape(bd, tn, DIFF_HEADS, DIFF_V_DIM),
             bk.reshape(bd, tn, DSA_HEADS, DSA_HEAD_DIM), bv.reshape(bd, tn, DSA_HEADS, DSA_HEAD_DIM),
             tail.reshape(bd, tn, LANES)[:, :, :IDX_DIM])
    return mix_p, mix_s, w_out, new_p, new_s


def _mlstm_layer(y, bsz, t, w_in, b_i, b_f, g_head, g_mix, c0, n0, m0):
    d = y.shape[1]
    nh = MLSTM_HEADS
    dk = d // (2 * nh)
    dv = d // nh
    segs = ((0, nh * dk, "plain"), (nh * dk, nh * dk, "plain"), (2 * nh * dk, nh * dv, "plain"),
            (2 * nh * dk + nh * dv, nh * dv, "plain"), (2 * nh * dk + 2 * nh * dv, LANES, "plain"))
    q, k, v, og, gts = _norm_proj(y, g_mix, w_in, segs)
    r3 = lambda x: x.reshape(bsz, t, x.shape[-1])
    hh, c, n, m = _mlstm(r3(q), r3(k), r3(v), r3(og), r3(gts), b_i, b_f, g_head, c0, n0, m0)
    return [hh.reshape(bsz * t, nh * dv)], (c, n, m)


def kernel(x_prompt, x_sample, cache_diff_k, cache_diff_v, cache_dsa_k, cache_dsa_v, cache_idx_k, state_mlstm_c, state_mlstm_n, state_mlstm_m, page_table, g_mix, g_ffn, g_final, w_attn_in, w_attn_out, lambda_q1, lambda_k1, lambda_q2, lambda_k2, g_diff_head, w_mlstm_in, b_mlstm_i, b_mlstm_f, g_mlstm_head, w_mlstm_out, w_ffn_up, w_ffn_down):
    b, t, d = x_prompt.shape
    bd, tn, _ = x_sample.shape
    depth = g_mix.shape[0]
    past = page_table.shape[1] * cache_idx_k.shape[2]
    nh = MLSTM_HEADS
    yp = x_prompt.reshape(b * t, d)
    ys = x_sample.reshape(bd * tn, d)
    new_p, new_s, st_p, st_s = [], [], [], []
    for layer in range(depth):
        li = layer // 2
        last = layer == depth - 1
        if layer % 2 == 0:
            lam_init = 0.8 - 0.6 * math.exp(-0.3 * layer)
            lam_rows = jnp.stack([lambda_q1[li], lambda_k1[li], lambda_q2[li], lambda_k2[li]]).astype(F32)
            mix_p, mix_s, w_out, np_, ns_ = _attn_layer(
                yp, ys, w_attn_in[li], w_attn_out[li], lam_rows, g_diff_head[li], g_mix[layer], lam_init,
                (cache_diff_k, cache_diff_v, cache_dsa_k, cache_dsa_v, cache_idx_k), page_table, li,
                (b, t, bd, tn, past))
            new_p.append(np_)
            new_s.append(ns_)
        else:
            w_in = _pad_cols(w_mlstm_in[li].astype(BF16))
            w_out = w_mlstm_out[li].astype(BF16)
            dk = d // (2 * nh)
            dv = d // nh
            zc = jnp.zeros((b, nh, dv, dk), F32)
            zn = jnp.zeros((b, nh, dk), F32)
            zm = jnp.zeros((b, nh), F32)
            mix_p, sp = _mlstm_layer(yp, b, t, w_in, b_mlstm_i[li], b_mlstm_f[li], g_mlstm_head[li],
                                     g_mix[layer], zc, zn, zm)
            mix_s, ss = _mlstm_layer(ys, bd, tn, w_in, b_mlstm_i[li], b_mlstm_f[li], g_mlstm_head[li],
                                     g_mix[layer], state_mlstm_c[li], state_mlstm_n[li], state_mlstm_m[li])
            st_p.append(sp)
            st_s.append(ss)
        w_up = w_ffn_up[layer].astype(BF16)
        w_down = w_ffn_down[layer].astype(BF16)
        gf = g_final if last else None
        yp = _mix_ffn(mix_p, w_out, yp, g_ffn[layer], w_up, w_down, gf)
        ys = _mix_ffn(mix_s, w_out, ys, g_ffn[layer], w_up, w_down, gf)
    stack = lambda items, idx: jnp.stack([it[idx] for it in items])
    return (yp.reshape(b, t, d), ys.reshape(bd, tn, d),
            stack(new_p, 0), stack(new_p, 1), stack(new_p, 2), stack(new_p, 3), stack(new_p, 4),
            stack(st_p, 0), stack(st_p, 1), stack(st_p, 2),
            stack(new_s, 0), stack(new_s, 1), stack(new_s, 2), stack(new_s, 3), stack(new_s, 4),
            stack(st_s, 0), stack(st_s, 1), stack(st_s, 2))
```

```python
import functools
import math

import jax
import jax.numpy as jnp
from jax import lax
from jax.experimental import pallas as pl
from jax.experimental.pallas import tpu as pltpu

F32 = jnp.float32
BF16 = jnp.bfloat16
I32 = jnp.int32

DIFF_HEADS = 4
DIFF_QK_DIM = 64
DIFF_V_DIM = 128
DSA_HEADS = 8
DSA_HEAD_DIM = 64
IDX_HEADS = 4
IDX_DIM = 64
DSA_TOPK = 256
MLSTM_HEADS = 4
MLSTM_CHUNK = 64
IGATE_SOFTCAP = 15.0
ROPE_THETA = 10000.0
RMS_EPS = 1e-6
HEAD_W = DIFF_HEADS * 2 * DIFF_QK_DIM

LANES = 128
SUBLANES = 8
VMEM_LIMIT = 56 * 1024 * 1024

INT_MIN = -(2 ** 31)
NEG_INF = float("-inf")


def _cparams(sem):
    return pltpu.CompilerParams(dimension_semantics=sem, vmem_limit_bytes=VMEM_LIMIT)


def _rms(x, g):
    return x * lax.rsqrt(jnp.mean(x * x, axis=-1, keepdims=True) + RMS_EPS) * g


def _dot(a, b):
    return jnp.dot(a, b, preferred_element_type=F32)


def _dot_nt(a, b):
    return lax.dot_general(a, b, (((1,), (1,)), ((), ())), preferred_element_type=F32)


def _dot_tn(a, b):
    return lax.dot_general(a, b, (((0,), (0,)), ((), ())), preferred_element_type=F32)


def _norm_proj_kernel(x_ref, g_ref, w_ref, cos_ref, sin_ref, *out_refs, segs, t_segs, tm):
    x = x_ref[...]
    h = _rms(x, g_ref[...]).astype(BF16)
    lane = lax.broadcasted_iota(I32, (tm, LANES), 1)
    first_half = (lane & (IDX_DIM // 2)) == 0
    t_refs = dict(zip(t_segs, out_refs[len(segs):]))
    for si, ((c0, width, mode), o_ref) in enumerate(zip(segs, out_refs)):
        y = _dot(h, w_ref[:, c0:c0 + width])
        for c in range(0, width, LANES):
            r = y[:, c:c + LANES]
            if mode != "plain":
                partner = jnp.where(first_half, pltpu.roll(r, LANES - 32, 1), pltpu.roll(r, 32, 1))
                rot = r * cos_ref[...] + partner * sin_ref[...]
                if mode == "tail":
                    rot = jnp.where(lane < IDX_DIM, rot, r * (IDX_HEADS ** -0.5))
                r = rot
            o_ref[:, c:c + LANES] = r
            if si in t_refs:
                rt = r.T
                if mode == "tail":
                    t_refs[si][0] = rt[:IDX_DIM]
                else:
                    t_refs[si][0, c:c + LANES, :] = rt


def _norm_proj(x, g, w, segs, cos=None, sin=None, t_segs=(), tm=512):
    n, d = x.shape
    tm = min(tm, n)
    assert n % tm == 0
    if cos is None:
        cos = jnp.zeros((tm, LANES), F32)
        sin = cos
    p = cos.shape[0]
    assert p % tm == 0 and n % p == 0
    nper = p // tm
    kern = functools.partial(_norm_proj_kernel, segs=tuple(segs), t_segs=tuple(t_segs), tm=tm)
    t_width = lambda si: IDX_DIM if segs[si][2] == "tail" else segs[si][1]
    return pl.pallas_call(
        kern,
        grid=(n // tm,),
        in_specs=[
            pl.BlockSpec((tm, d), lambda i: (i, 0)),
            pl.BlockSpec((1, d), lambda i: (0, 0)),
            pl.BlockSpec(w.shape, lambda i: (0, 0)),
            pl.BlockSpec((tm, LANES), lambda i: (i % nper, 0)),
            pl.BlockSpec((tm, LANES), lambda i: (i % nper, 0)),
        ],
        out_specs=[pl.BlockSpec((tm, wd), lambda i: (i, 0)) for (_, wd, _) in segs]
        + [pl.BlockSpec((1, t_width(si), tm), lambda i: (i // nper, 0, i % nper)) for si in t_segs],
        out_shape=[jax.ShapeDtypeStruct((n, wd), F32) for (_, wd, _) in segs]
        + [jax.ShapeDtypeStruct((n // p, t_width(si), p), F32) for si in t_segs],
        compiler_params=_cparams(("parallel",)),
    )(x, g.reshape(1, d), w, cos, sin)


def _rope_tables(pos):
    half = IDX_DIM // 2
    inv = ROPE_THETA ** (-jnp.arange(half, dtype=F32) / half)
    ang = pos.astype(F32)[:, None] * inv[None, :]
    cos = jnp.cos(ang)
    sin = jnp.sin(ang)
    cos_t = jnp.tile(cos, (1, LANES // half))
    sin_t = jnp.tile(jnp.concatenate([-sin, sin], axis=1), (1, LANES // IDX_DIM))
    return cos_t, sin_t


def _mix_ffn_kernel(*refs, widths, final_norm):
    n_in = len(widths)
    x_refs = refs[:n_in]
    wo_ref, res_ref, g_ref, wu_ref, wd_ref, gf_ref, o_ref, h_scr, acc_scr = refs[n_in:]
    f = pl.program_id(1)

    @pl.when(f == 0)
    def _():
        y = res_ref[...]
        r0 = 0
        for x_ref, wd in zip(x_refs, widths):
            y = y + _dot(x_ref[...].astype(BF16), wo_ref[r0:r0 + wd, :])
            r0 += wd
        h_scr[...] = _rms(y, g_ref[...]).astype(BF16)
        acc_scr[...] = y

    u = jnp.maximum(_dot(h_scr[...], wu_ref[...]), 0.0)
    acc_scr[...] += _dot((u * u).astype(BF16), wd_ref[...])

    @pl.when(f == pl.num_programs(1) - 1)
    def _():
        y = acc_scr[...]
        if final_norm:
            y = _rms(y, gf_ref[...])
        o_ref[...] = y


def _mix_ffn(xs, w_out, res, g, w_up, w_down, g_final=None, tm=512, tf=2048):
    n, d = res.shape
    dff = w_up.shape[1]
    tm = min(tm, n)
    widths = tuple(x.shape[1] for x in xs)
    final_norm = g_final is not None
    if g_final is None:
        g_final = g
    kern = functools.partial(_mix_ffn_kernel, widths=widths, final_norm=final_norm)
    row = lambda wd: pl.BlockSpec((tm, wd), lambda i, f: (i, 0))
    vec = pl.BlockSpec((1, d), lambda i, f: (0, 0))
    return pl.pallas_call(
        kern,
        grid=(n // tm, dff // tf),
        in_specs=[row(wd) for wd in widths] + [
            pl.BlockSpec(w_out.shape, lambda i, f: (0, 0)),
            row(d),
            vec,
            pl.BlockSpec((d, tf), lambda i, f: (0, f)),
            pl.BlockSpec((tf, d), lambda i, f: (f, 0)),
            vec,
        ],
        out_specs=row(d),
        out_shape=jax.ShapeDtypeStruct((n, d), F32),
        scratch_shapes=[pltpu.VMEM((tm, d), BF16), pltpu.VMEM((tm, d), F32)],
        compiler_params=_cparams(("parallel", "arbitrary")),
    )(*xs, w_out, res, g.reshape(1, d), w_up, w_down, g_final.reshape(1, d))


def _lambda(lam_ref, lam_init):
    l = lam_ref[...]
    a = jnp.sum(l[0:1] * l[1:2], axis=1, keepdims=True)
    b = jnp.sum(l[2:3] * l[3:4], axis=1, keepdims=True)
    return jnp.exp(a) - jnp.exp(b) + lam_init


def _lane_tile(x, n):
    return x if n == 1 else jnp.concatenate([x] * n, axis=1)


def _split_halves(q):
    lane = lax.broadcasted_iota(I32, q.shape, 1)
    lo = lane < DSA_HEAD_DIM
    return jnp.concatenate([jnp.where(lo, q, 0.0), jnp.where(lo, 0.0, q)], axis=0)


def _key_to_float(key):
    return pltpu.bitcast(jnp.where(key >= 0, key, key ^ jnp.int32(0x7FFFFFFF)), F32)


def _kth_largest(count_ge, rows, kk):
    def body(it, t):
        cand = t + jnp.left_shift(jnp.int32(1), 31 - it)
        return jnp.where(count_ge(_key_to_float(cand)) >= kk, cand, t)

    return _key_to_float(lax.fori_loop(0, 32, body, jnp.full((rows, 1), INT_MIN, I32)))


def _diff_attn_kernel(lam_ref, g_ref, q_ref, k_ref, v_ref, o_ref, m_scr, l_scr, acc_scr, *, blk, lam_init):
    i = pl.program_id(2)
    q2 = _split_halves(q_ref[0] * (DIFF_QK_DIM ** -0.5)).astype(BF16)
    m_scr[...] = jnp.full(m_scr.shape, NEG_INF, F32)
    l_scr[...] = jnp.zeros(l_scr.shape, F32)
    acc_scr[...] = jnp.zeros(acc_scr.shape, F32)

    def step(j, on_diagonal):
        start = pl.multiple_of(j * blk, blk)
        kj = k_ref[0, pl.ds(start, blk), :].astype(BF16)
        vj = v_ref[0, pl.ds(start, blk), :].astype(BF16)
        s = _dot_nt(q2, kj)
        if on_diagonal:
            row = lax.broadcasted_iota(I32, (2 * blk, blk), 0)
            col = lax.broadcasted_iota(I32, (2 * blk, blk), 1)
            s = jnp.where(col <= jnp.where(row >= blk, row - blk, row), s, NEG_INF)
        m_old = m_scr[...]
        m_new = jnp.maximum(m_old, jnp.max(s, axis=1, keepdims=True))
        alpha = jnp.exp(m_old - m_new)
        p = jnp.exp(s - _lane_tile(m_new, blk // LANES))
        l_scr[...] = alpha * l_scr[...] + jnp.sum(p, axis=1, keepdims=True)
        acc_scr[...] = alpha * acc_scr[...] + _dot(p.astype(BF16), vj)
        m_scr[...] = m_new

    def body(j, carry):
        step(j, False)
        return carry

    lax.fori_loop(0, i, body, 0)
    step(i, True)
    o = acc_scr[...] / l_scr[...]
    a = o[:blk] - _lambda(lam_ref, lam_init) * o[blk:]
    o_ref[0] = _rms(a, g_ref[...]) * (1.0 - lam_init)


def _diff_attn_prompt(q, k, v, lam_rows, g_head, lam_init, blk=512):
    b, t, _ = q.shape
    blk = min(blk, t)
    kern = functools.partial(_diff_attn_kernel, blk=blk, lam_init=lam_init)
    return pl.pallas_call(
        kern,
        grid=(b, DIFF_HEADS, t // blk),
        in_specs=[
            pl.BlockSpec(lam_rows.shape, lambda bi, h, i: (0, 0)),
            pl.BlockSpec((1, DIFF_V_DIM), lambda bi, h, i: (0, 0)),
            pl.BlockSpec((1, blk, LANES), lambda bi, h, i: (bi, i, h)),
            pl.BlockSpec((1, t, LANES), lambda bi, h, i: (bi, 0, h)),
            pl.BlockSpec((1, t, LANES), lambda bi, h, i: (bi, 0, h)),
        ],
        out_specs=pl.BlockSpec((1, blk, LANES), lambda bi, h, i: (bi, i, h)),
        out_shape=jax.ShapeDtypeStruct((b, t, HEAD_W), F32),
        scratch_shapes=[pltpu.VMEM((2 * blk, LANES), F32), pltpu.VMEM((2 * blk, LANES), F32),
                        pltpu.VMEM((2 * blk, LANES), F32)],
        compiler_params=_cparams(("parallel", "parallel", "parallel")),
    )(lam_rows, g_head.reshape(1, DIFF_V_DIM), q, k, v)


def _dsa_prompt_kernel(q_ref, iq_ref, tail_q_ref, k_ref, v_ref, tail_k_ref, o_ref, key_scr, bias_scr,
                       *, qb, t, topk, wstep):
    i = pl.program_id(1)

    def body(w):
        qpos = i * qb + lax.broadcasted_iota(I32, (qb, w), 0)
        kpos = lax.broadcasted_iota(I32, (qb, w), 1)
        causal = kpos <= qpos

        ik = tail_k_ref[0, :w, :IDX_DIM].astype(BF16)
        tq = tail_q_ref[0]
        score = jnp.zeros((qb, w), F32)
        for h in range(IDX_HEADS):
            qh = (iq_ref[0, :, h * IDX_DIM:(h + 1) * IDX_DIM] * (IDX_DIM ** -0.5)).astype(BF16)
            score = score + jnp.maximum(_dot_nt(qh, ik), 0.0) * tq[:, IDX_DIM + h:IDX_DIM + h + 1]
        key_scr[:, :w] = jnp.where(causal, score, NEG_INF)

        half = qb // 2

        def partial_count(r0, cand_key):
            hit = jnp.where(key_scr[r0:r0 + half, :w] >= _key_to_float(cand_key), 1.0, 0.0)
            acc = hit[:, :LANES]
            for g in range(LANES, w, LANES):
                acc = acc + hit[:, g:g + LANES]
            return acc

        def search_step(it, carry):
            t_a, t_b, part_b = carry
            bit = jnp.left_shift(jnp.int32(1), 31 - it)
            cnt_b = jnp.sum(part_b, axis=1, keepdims=True)
            part_a = partial_count(0, t_a + bit)
            t_b = jnp.where(cnt_b >= float(topk), t_b + bit, t_b)
            cnt_a = jnp.sum(part_a, axis=1, keepdims=True)
            next_bit = jnp.left_shift(jnp.int32(1), jnp.maximum(30 - it, 0))
            part_b = partial_count(half, t_b + next_bit)
            t_a = jnp.where(cnt_a >= float(topk), t_a + bit, t_a)
            return t_a, t_b, part_b

        t0 = jnp.full((half, 1), INT_MIN, I32)
        t_a, t_b, _ = lax.fori_loop(0, 32, search_step, (t0, t0, partial_count(half, t0 + jnp.int32(INT_MIN))))
        thr = _key_to_float(jnp.concatenate([t_a, t_b], axis=0))
        key = key_scr[:, :w]
        above = (key > thr) | (qpos < topk)
        n_above = jnp.sum(jnp.where(above, 1.0, 0.0), axis=1, keepdims=True)
        room = float(topk) - n_above
        tie = key == thr
        tri = (lax.broadcasted_iota(I32, (LANES, LANES), 0) <= lax.broadcasted_iota(I32, (LANES, LANES), 1))
        tri = jnp.where(tri, 1.0, 0.0).astype(BF16)
        before = jnp.zeros((qb, 1), F32)
        for c in range(0, w, LANES):
            tc = tie[:, c:c + LANES]
            rank = _dot(jnp.where(tc, 1.0, 0.0).astype(BF16), tri) + before
            sel = (above[:, c:c + LANES] | (tc & (rank <= room))) & causal[:, c:c + LANES]
            bias_scr[:, c:c + LANES] = jnp.where(sel, 0.0, NEG_INF)
            before = rank[:, LANES - 1:LANES]

        lane = lax.broadcasted_iota(I32, (qb, LANES), 1)
        for pb in range(DSA_HEADS // 2):
            cols = slice(pb * LANES, (pb + 1) * LANES)
            q2 = _split_halves(q_ref[0, :, cols] * (DSA_HEAD_DIM ** -0.5)).astype(BF16)
            kp = k_ref[0, :w, cols].astype(BF16)
            vp = v_ref[0, :w, cols].astype(BF16)
            outs = []
            for m in range(2):
                s = _dot_nt(q2[m * qb:(m + 1) * qb], kp) + bias_scr[:, :w]
                mx = jnp.max(s, axis=1, keepdims=True)
                p = jnp.exp(s - mx)
                den = jnp.sum(p, axis=1, keepdims=True)
                outs.append(_dot(p.astype(BF16), vp) / den)
            o_ref[0, :, cols] = jnp.where(lane < DSA_HEAD_DIM, outs[0], outs[1])

    n_w = t // wstep
    needed = ((i + 1) * qb + wstep - 1) // wstep
    for a in range(1, n_w + 1):
        pl.when(needed == a)(functools.partial(body, a * wstep))


def _dsa_prompt(q, k, v, iq, tail, qb=256, wstep=512):
    b, t, _ = q.shape
    qb = min(qb, t)
    wstep = min(wstep, t)
    assert t % wstep == 0 and wstep % qb == 0
    topk = min(DSA_TOPK, t // 4)
    kern = functools.partial(_dsa_prompt_kernel, qb=qb, t=t, topk=topk, wstep=wstep)
    return pl.pallas_call(
        kern,
        grid=(b, t // qb),
        in_specs=[
            pl.BlockSpec((1, qb, HEAD_W), lambda bi, i: (bi, i, 0)),
            pl.BlockSpec((1, qb, IDX_HEADS * IDX_DIM), lambda bi, i: (bi, i, 0)),
            pl.BlockSpec((1, qb, LANES), lambda bi, i: (bi, i, 0)),
            pl.BlockSpec((1, t, HEAD_W), lambda bi, i: (bi, 0, 0)),
            pl.BlockSpec((1, t, HEAD_W), lambda bi, i: (bi, 0, 0)),
            pl.BlockSpec((1, t, LANES), lambda bi, i: (bi, 0, 0)),
        ],
        out_specs=pl.BlockSpec((1, qb, HEAD_W), lambda bi, i: (bi, i, 0)),
        out_shape=jax.ShapeDtypeStruct((b, t, HEAD_W), F32),
        scratch_shapes=[pltpu.VMEM((qb, t), F32), pltpu.VMEM((qb, t), F32)],
        compiler_params=_cparams(("parallel", "parallel")),
    )(q, iq, tail, k, v, tail)


def _fetch_pages(pt_ref, li, pg, streams, sem):
    b = pl.program_id(0)
    j = pl.program_id(1)
    n_j = pl.num_programs(1)
    step = b * n_j + j
    slot = step % 2

    def page_copies(bb, jj, sl):
        cps = []
        for g in range(pg):
            pid = pt_ref[bb, jj * pg + g]
            for k, (src, buf) in enumerate(streams):
                cps.append(pltpu.make_async_copy(src.at[li, pid], buf.at[sl, g], sem.at[sl, k]))
        return cps

    def start_all(cps):
        for n, cp in enumerate(cps):
            cp.start(priority=n % 2)

    @pl.when(step == 0)
    def _():
        start_all(page_copies(b, j, slot))

    @pl.when(step + 1 < pl.num_programs(0) * n_j)
    def _():
        wrap = j + 1 == n_j
        start_all(page_copies(jnp.where(wrap, b + 1, b), jnp.where(wrap, 0, j + 1), 1 - slot))

    for cp in page_copies(b, j, slot):
        cp.wait()
    return slot


def _dsa_sample_select_kernel(pt_ref, iq_ref, tail_ref, cik_ref, bias_ref, pbuf, sem, key_scr,
                              *, pg, tn, n_chunks, topk, li):
    j = pl.program_id(1)
    slot = _fetch_pages(pt_ref, li, pg, [(cik_ref, pbuf)], sem)
    tail = tail_ref[0]
    iq = iq_ref[0]
    q_rows = jnp.concatenate([iq[:, h * IDX_DIM:(h + 1) * IDX_DIM] for h in range(IDX_HEADS)], axis=0)
    q_rows = (q_rows * (IDX_DIM ** -0.5)).astype(BF16)
    w_rows = jnp.concatenate([tail[:, IDX_DIM + h:IDX_DIM + h + 1] for h in range(IDX_HEADS)], axis=0)

    def scores(dots):
        d = jnp.maximum(dots, 0.0) * w_rows
        s = d[0:tn]
        for h in range(1, IDX_HEADS):
            s = s + d[h * tn:(h + 1) * tn]
        return s

    k_step = jnp.concatenate([pbuf[slot, g] for g in range(pg)], axis=1).astype(BF16)
    s_step = scores(_dot(q_rows, k_step))
    for g in range(pg):
        key_scr[j * pg + g] = s_step[:, g * LANES:(g + 1) * LANES]

    @pl.when(j == pl.num_programs(1) - 1)
    def _():
        n_past = n_chunks - 1
        k_new = jnp.concatenate([tail[:, :IDX_DIM], jnp.zeros((LANES - tn, IDX_DIM), F32)], axis=0).astype(BF16)
        s_new = scores(_dot_nt(q_rows, k_new))
        row = lax.broadcasted_iota(I32, (tn, LANES), 0)
        col = lax.broadcasted_iota(I32, (tn, LANES), 1)
        key_scr[n_past] = jnp.where(col <= row, s_new, NEG_INF)

        def count_ge(cand):
            n_part = min(16, n_past)
            per = n_past // n_part
            tot = jnp.where(key_scr[n_past] >= cand, 1.0, 0.0)
            for a in range(n_part):
                hi = n_past if a == n_part - 1 else (a + 1) * per
                tot = tot + jnp.sum(jnp.where(key_scr[a * per:hi] >= cand[None], 1.0, 0.0), axis=0)
            return jnp.sum(tot, axis=1, keepdims=True)

        thr = _kth_largest(count_ge, tn, float(topk))
        key = key_scr[...]
        n_valid = n_past * LANES + 1 + lax.broadcasted_iota(I32, (tn, 1), 0)
        above = (key > thr[None]) | (n_valid <= topk)[None]
        n_above = jnp.sum(jnp.sum(jnp.where(above, 1.0, 0.0), axis=0), axis=1, keepdims=True)
        room = float(topk) - n_above
        tie = jnp.where(key == thr[None], 1.0, 0.0)
        tri = (lax.broadcasted_iota(I32, (LANES, LANES), 0) <= lax.broadcasted_iota(I32, (LANES, LANES), 1))
        tri = jnp.where(tri, 1.0, 0.0).astype(BF16)
        within = _dot(tie.reshape(n_chunks * tn, LANES).astype(BF16), tri).reshape(n_chunks, tn, LANES)
        totals = within[:, :, LANES - 1:LANES]

        def chunk_body(c, before):
            rank = within[c] + before
            sel = above[c] | ((tie[c] > 0) & (rank <= room))
            bias_ref[0, c] = jnp.where(sel, 0.0, NEG_INF)
            return before + totals[c]

        before = jnp.zeros((tn, 1), F32)
        for c in range(n_chunks):
            before = chunk_body(c, before)
        bias_ref[0, n_past] = jnp.where(col <= row, bias_ref[0, n_past], NEG_INF)


def _dsa_sample_select(iq, tail, cache_ik_t, page_table, li, pg=32):
    bd, tn, _ = iq.shape
    n_pages = page_table.shape[1]
    page = cache_ik_t.shape[3]
    pg = min(pg, n_pages)
    assert page == LANES and n_pages % pg == 0
    n_chunks = n_pages + 1
    topk = min(DSA_TOPK, (n_pages * page + tn) // 4)
    kern = functools.partial(_dsa_sample_select_kernel, pg=pg, tn=tn, n_chunks=n_chunks, topk=topk, li=li)
    return pl.pallas_call(
        kern,
        grid_spec=pltpu.PrefetchScalarGridSpec(
            num_scalar_prefetch=1,
            grid=(bd, n_pages // pg),
            in_specs=[
                pl.BlockSpec((1, tn, IDX_HEADS * IDX_DIM), lambda b, j, pt: (b, 0, 0)),
                pl.BlockSpec((1, tn, LANES), lambda b, j, pt: (b, 0, 0)),
                pl.BlockSpec(memory_space=pl.ANY),
            ],
            out_specs=pl.BlockSpec((1, n_chunks, tn, LANES), lambda b, j, pt: (b, 0, 0, 0)),
            scratch_shapes=[pltpu.VMEM((2, pg, IDX_DIM, page), F32), pltpu.SemaphoreType.DMA((2, 1)),
                            pltpu.VMEM((n_chunks, tn, LANES), F32)],
        ),
        out_shape=jax.ShapeDtypeStruct((bd, n_chunks, tn, LANES), F32),
        compiler_params=_cparams(("arbitrary", "arbitrary")),
    )(page_table, iq, tail, cache_ik_t)


def _paged_attn_kernel(pt_ref, lam_ref, g_ref, q_ref, kn_ref, vn_ref, bias_ref, ck_ref, cv_ref, o_ref,
                       kbuf, vbuf, sem, m_scr, l_scr, acc_scr, *, pg, tn, diff, lam_init, li):
    j = pl.program_id(1)
    nblk = HEAD_W // LANES
    rows = 2 * tn
    scale = DIFF_QK_DIM ** -0.5
    slot = _fetch_pages(pt_ref, li, pg, [(ck_ref, kbuf), (cv_ref, vbuf)], sem)

    def update(blk, s_parts, pv_fn):
        s = jnp.concatenate(s_parts, axis=1)
        m_old = m_scr[blk]
        m_new = jnp.maximum(m_old, jnp.max(s, axis=1, keepdims=True))
        alpha = jnp.exp(m_old - m_new)
        p = jnp.exp(s - _lane_tile(m_new, len(s_parts)))
        l_scr[blk] = alpha * l_scr[blk] + jnp.sum(p, axis=1, keepdims=True)
        pv = None
        for g in range(len(s_parts)):
            term = pv_fn(g, p[:, g * LANES:(g + 1) * LANES].astype(BF16))
            pv = term if pv is None else pv + term
        acc_scr[blk] = alpha * acc_scr[blk] + pv
        m_scr[blk] = m_new

    q_all = q_ref[0] * scale
    q2 = [_split_halves(q_all[:, b * LANES:(b + 1) * LANES]).astype(BF16) for b in range(nblk)]

    @pl.when(j == 0)
    def _():
        m_scr[...] = jnp.full(m_scr.shape, jnp.finfo(F32).min, F32)
        l_scr[...] = jnp.zeros(l_scr.shape, F32)
        acc_scr[...] = jnp.zeros(acc_scr.shape, F32)
        n_past = bias_ref.shape[1] - 1
        row = lax.broadcasted_iota(I32, (rows, LANES), 0)
        col = lax.broadcasted_iota(I32, (rows, LANES), 1)
        causal = col <= jnp.where(row >= tn, row - tn, row)
        bn = bias_ref[0, n_past]
        bn2 = jnp.concatenate([bn, bn], axis=0)
        pad = jnp.zeros((LANES - tn, LANES), F32)
        for blk in range(nblk):
            cols = slice(blk * LANES, (blk + 1) * LANES)
            kn = jnp.concatenate([kn_ref[0, :, cols], pad], axis=0).astype(BF16)
            vn = jnp.concatenate([vn_ref[0, :, cols], pad], axis=0).astype(BF16)
            s = jnp.where(causal, _dot_nt(q2[blk], kn), NEG_INF)
            if not diff:
                s = s + bn2
            update(blk, [s], lambda g, p: _dot(p, vn))

    if not diff:
        bias2 = []
        for g in range(pg):
            bp = bias_ref[0, j * pg + g]
            bias2.append(jnp.concatenate([bp, bp], axis=0))
    for blk in range(nblk):
        rows_blk = slice(blk * LANES, (blk + 1) * LANES)
        s_parts = [_dot(q2[blk], kbuf[slot, g, rows_blk, :].astype(BF16)) for g in range(pg)]
        if diff:
            update(blk, s_parts,
                   lambda g, p: _dot(p, vbuf[slot, g, pl.ds(blk, LANES, stride=nblk), :].astype(BF16)))
        else:
            update(blk, [s + b2 for s, b2 in zip(s_parts, bias2)],
                   lambda g, p: _dot_nt(p, vbuf[slot, g, rows_blk, :].astype(BF16)))

    @pl.when(j == pl.num_programs(1) - 1)
    def _():
        lane = lax.broadcasted_iota(I32, (tn, LANES), 1)
        for blk in range(nblk):
            o = acc_scr[blk] / l_scr[blk]
            if diff:
                a = o[:tn] - _lambda(lam_ref, lam_init) * o[tn:]
                r = _rms(a, g_ref[...]) * (1.0 - lam_init)
            else:
                r = jnp.where(lane < DSA_HEAD_DIM, o[:tn], o[tn:])
            o_ref[0, :, blk * LANES:(blk + 1) * LANES] = r


def _paged_attn(q, k_new, v_new, bias, cache_k, cache_v, page_table, li, lam_rows, g_head, lam_init, diff, pg=16):
    bd, tn, _ = q.shape
    n_pages = page_table.shape[1]
    pg = min(pg, n_pages)
    assert n_pages % pg == 0 and cache_k.shape[2:] == (HEAD_W, LANES) and cache_v.shape[2:] == (HEAD_W, LANES)
    n_chunks = n_pages + 1
    kern = functools.partial(_paged_attn_kernel, pg=pg, tn=tn, diff=diff, lam_init=lam_init, li=li)
    nblk = HEAD_W // LANES
    tok = pl.BlockSpec((1, tn, HEAD_W), lambda b, j, pt: (b, 0, 0))
    stats = pltpu.VMEM((nblk, 2 * tn, LANES), F32)
    pages = pltpu.VMEM((2, pg, HEAD_W, LANES), F32)
    return pl.pallas_call(
        kern,
        grid_spec=pltpu.PrefetchScalarGridSpec(
            num_scalar_prefetch=1,
            grid=(bd, n_pages // pg),
            in_specs=[
                pl.BlockSpec(lam_rows.shape, lambda b, j, pt: (0, 0)),
                pl.BlockSpec((1, DIFF_V_DIM), lambda b, j, pt: (0, 0)),
                tok, tok, tok,
                pl.BlockSpec((1, n_chunks, tn, LANES), lambda b, j, pt: (b, 0, 0, 0)),
                pl.BlockSpec(memory_space=pl.ANY),
                pl.BlockSpec(memory_space=pl.ANY),
            ],
            out_specs=pl.BlockSpec((1, tn, HEAD_W), lambda b, j, pt: (b, 0, 0)),
            scratch_shapes=[pages, pages, pltpu.SemaphoreType.DMA((2, 2)), stats, stats, stats],
        ),
        out_shape=jax.ShapeDtypeStruct((bd, tn, HEAD_W), F32),
        compiler_params=_cparams(("arbitrary", "arbitrary")),
    )(page_table, lam_rows, g_head.reshape(1, DIFF_V_DIM), q, k_new, v_new, bias, cache_k, cache_v)


def _mlstm_kernel(q_ref, k_ref, v_ref, og_ref, gcol_ref, grow_ref, bcol_ref, brow_ref, gh_ref, c0_ref, n0_ref, m0_ref,
                  h_ref, c_out, n_out, m_out, c_scr, n_scr, m_scr, *, chunk, n_sub, dk, dv, gb):
    j = pl.program_id(1)
    nh = MLSTM_HEADS

    @pl.when(j == 0)
    def _():
        for bb in range(gb):
            for h in range(nh):
                c_scr[bb * nh + h] = c0_ref[bb, h]
                n_scr[bb * nh + h] = n0_ref[bb, h:h + 1, :]
                m_scr[bb * nh + h] = m0_ref[bb, :, h:h + 1]

    r_i = lax.broadcasted_iota(I32, (chunk, chunk), 0)
    c_i = lax.broadcasted_iota(I32, (chunk, chunk), 1)
    causal = c_i <= r_i

    def gates(x, is_input_gate):
        ig = IGATE_SOFTCAP * jnp.tanh(x / IGATE_SOFTCAP)
        lf = jnp.minimum(x, 0.0) - jnp.log1p(jnp.exp(-jnp.abs(x)))
        return jnp.where(is_input_gate, ig, lf)

    def chunk_body(s, carry):
        r0 = pl.multiple_of(s * chunk, chunk)
        for bb in range(gb):
            gc = gcol_ref[bb, s] + bcol_ref[...]
            gr = grow_ref[bb, s] + brow_ref[...]
            gc = gates(gc, lax.broadcasted_iota(I32, gc.shape, 1) < nh)
            gr = gates(gr, lax.broadcasted_iota(I32, gr.shape, 0) < nh)
            for h in range(nh):
                si = bb * nh + h
                m_prev = m_scr[si]
                c_prev = c_scr[si]
                n_prev = n_scr[si]
                qc = q_ref[bb, pl.ds(r0, chunk), h * dk:(h + 1) * dk] * (dk ** -0.5)
                kc = k_ref[bb, pl.ds(r0, chunk), h * dk:(h + 1) * dk]
                vc = v_ref[bb, pl.ds(r0, chunk), h * dv:(h + 1) * dv].astype(BF16)
                qb16 = qc.astype(BF16)
                qk = _dot_nt(qb16, kc.astype(BF16))
                q_c = _dot_nt(qb16, c_prev.astype(BF16))
                i_col, f_col = gc[:, h:h + 1], gc[:, nh + h:nh + h + 1]
                i_row, f_row = gr[h:h + 1, :], gr[nh + h:nh + h + 1, :]
                b_col = jnp.sum(jnp.where(causal, f_row, 0.0), axis=1, keepdims=True)
                b_row = jnp.sum(jnp.where(r_i <= c_i, f_col, 0.0), axis=0, keepdims=True)
                dmat = jnp.where(causal, b_col - b_row + i_row, NEG_INF)
                inter = b_col + m_prev
                mt = jnp.maximum(inter, jnp.max(dmat, axis=1, keepdims=True))
                b_end = jnp.sum(f_row, axis=1, keepdims=True)
                d_end = b_end - b_col + i_col
                m_new = jnp.maximum(b_end + m_prev, jnp.max(d_end, axis=0, keepdims=True))
                kw = kc * jnp.exp(d_end - m_new)
                decay = jnp.exp(b_end + m_prev - m_new)
                c_scr[si] = decay * c_prev + _dot_tn(vc, kw.astype(BF16))
                n_scr[si] = decay * n_prev + jnp.sum(kw, axis=0, keepdims=True)
                m_scr[si] = m_new
                sm = qk * jnp.exp(dmat - mt)
                w_inter = jnp.exp(inter - mt)
                num = _dot(sm.astype(BF16), vc) + w_inter * q_c
                den = jnp.sum(sm, axis=1, keepdims=True) + w_inter * jnp.sum(qc * n_prev, axis=1, keepdims=True)
                hh = num / jnp.maximum(jnp.abs(den), jnp.exp(-mt))
                hn = _rms(hh, gh_ref[:, h * dv:(h + 1) * dv])
                og = og_ref[bb, pl.ds(r0, chunk), h * dv:(h + 1) * dv]
                h_ref[bb, pl.ds(r0, chunk), h * dv:(h + 1) * dv] = hn * (1.0 / (1.0 + jnp.exp(-og)))
        return carry

    lax.fori_loop(0, n_sub, chunk_body, 0)

    @pl.when(j == pl.num_programs(1) - 1)
    def _():
        for bb in range(gb):
            for h in range(nh):
                c_out[bb, h] = c_scr[bb * nh + h]
                n_out[bb, h:h + 1, :] = n_scr[bb * nh + h]
                m_out[bb, :, h:h + 1] = m_scr[bb * nh + h]


def _mlstm(q, k, v, og, gates_raw, b_i, b_f, g_head, c0, n0, m0, tb=512, gb=1):
    b, t, _ = q.shape
    nh = MLSTM_HEADS
    dk = q.shape[2] // nh
    dv = v.shape[2] // nh
    chunk = math.gcd(t, MLSTM_CHUNK)
    tb = min(tb, t)
    gb = math.gcd(gb, b)
    n_sub = tb // chunk
    nc = t // chunk
    g = gates_raw[:, :, :2 * nh].reshape(b, nc, chunk, 2 * nh)
    g_row = jnp.swapaxes(g, 2, 3)
    bias = jnp.concatenate([b_i, b_f]).astype(F32)
    kern = functools.partial(_mlstm_kernel, chunk=chunk, n_sub=n_sub, dk=dk, dv=dv, gb=gb)
    tok = lambda w: pl.BlockSpec((gb, tb, w), lambda bi, j: (bi, j, 0))
    outs = pl.pallas_call(
        kern,
        grid=(b // gb, t // tb),
        in_specs=[
            tok(nh * dk), tok(nh * dk), tok(nh * dv), tok(nh * dv),
            pl.BlockSpec((gb, n_sub, chunk, 2 * nh), lambda bi, j: (bi, j, 0, 0)),
            pl.BlockSpec((gb, n_sub, 2 * nh, chunk), lambda bi, j: (bi, j, 0, 0)),
            pl.BlockSpec((1, 2 * nh), lambda bi, j: (0, 0)),
            pl.BlockSpec((2 * nh, 1), lambda bi, j: (0, 0)),
            pl.BlockSpec((1, nh * dv), lambda bi, j: (0, 0)),
            pl.BlockSpec((gb, nh, dv, dk), lambda bi, j: (bi, 0, 0, 0)),
            pl.BlockSpec((gb, nh, dk), lambda bi, j: (bi, 0, 0)),
            pl.BlockSpec((gb, 1, nh), lambda bi, j: (bi, 0, 0)),
        ],
        out_specs=[
            tok(nh * dv),
            pl.BlockSpec((gb, nh, dv, dk), lambda bi, j: (bi, 0, 0, 0)),
            pl.BlockSpec((gb, nh, dk), lambda bi, j: (bi, 0, 0)),
            pl.BlockSpec((gb, 1, nh), lambda bi, j: (bi, 0, 0)),
        ],
        out_shape=[
            jax.ShapeDtypeStruct((b, t, nh * dv), F32),
            jax.ShapeDtypeStruct((b, nh, dv, dk), F32),
            jax.ShapeDtypeStruct((b, nh, dk), F32),
            jax.ShapeDtypeStruct((b, 1, nh), F32),
        ],
        scratch_shapes=[pltpu.VMEM((gb * nh, dv, dk), F32), pltpu.VMEM((gb * nh, 1, dk), F32),
                        pltpu.VMEM((gb * nh, 1, 1), F32)],
        compiler_params=_cparams(("parallel", "arbitrary")),
    )(q, k, v, og, g, g_row, bias.reshape(1, 2 * nh), bias.reshape(2 * nh, 1), g_head.reshape(1, nh * dv),
      c0, n0, m0.reshape(b, 1, nh))
    hh, c, n, m = outs
    return hh, c, n, m.reshape(b, nh)


_EVEN_SEGS = tuple([(i * HEAD_W, HEAD_W, mode) for i, mode in enumerate(("rope", "rope", "plain", "rope", "rope", "plain"))]
                   + [(6 * HEAD_W, IDX_HEADS * IDX_DIM, "rope"), (6 * HEAD_W + IDX_HEADS * IDX_DIM, LANES, "tail")])


def _pad_cols(w, mult=LANES):
    pad = (-w.shape[1]) % mult
    return jnp.pad(w, ((0, 0), (0, pad))) if pad else w


def _attn_layer(yp, ys, w_in, w_out, lam_rows, g_head, g_mix, lam_init, caches, page_table, li, dims):
    b, t, bd, tn, past = dims
    d = yp.shape[1]
    w_in = _pad_cols(w_in.astype(BF16))
    w_out = w_out.astype(BF16)
    cache_dk, cache_dv, cache_sk, cache_sv, cache_ik = caches

    cos, sin = _rope_tables(jnp.arange(t))
    aq, ak, av, bq, bk, bv, iq, tail, ak_t, bk_t, bv_t, ik_t = _norm_proj(
        yp, g_mix, w_in, _EVEN_SEGS, cos, sin, t_segs=(1, 4, 5, 7))
    r3 = lambda x: x.reshape(b, t, x.shape[-1])
    a_out = _diff_attn_prompt(r3(aq), r3(ak), r3(av), lam_rows, g_head, lam_init)
    b_out = _dsa_prompt(r3(bq), r3(bk), r3(bv), r3(iq), r3(tail))
    mix_p = [a_out.reshape(b * t, HEAD_W), b_out.reshape(b * t, HEAD_W)]
    new_p = (jnp.moveaxis(ak_t.reshape(b, DIFF_HEADS, 2, DIFF_QK_DIM, t), -1, 1),
             av.reshape(b, t, DIFF_HEADS, DIFF_V_DIM),
             jnp.moveaxis(bk_t.reshape(b, DSA_HEADS, DSA_HEAD_DIM, t), -1, 1),
             jnp.moveaxis(bv_t.reshape(b, DSA_HEADS, DSA_HEAD_DIM, t), -1, 1),
             jnp.moveaxis(ik_t, -1, 1))

    cos, sin = _rope_tables(jnp.tile(past + jnp.arange(tn), bd))
    aq, ak, av, bq, bk, bv, iq, tail = _norm_proj(ys, g_mix, w_in, _EVEN_SEGS, cos, sin)
    s3 = lambda x: x.reshape(bd, tn, x.shape[-1])
    npool, page = cache_dk.shape[1], cache_dk.shape[2]
    slot_last = lambda c: jnp.moveaxis(c, 2, -1).reshape(c.shape[0], npool, -1, page)
    bias = _dsa_sample_select(s3(iq), s3(tail), slot_last(cache_ik), page_table, li)
    dv_rows = cache_dv.reshape(cache_dv.shape[0], npool, page * DIFF_HEADS, DIFF_V_DIM)
    a_out = _paged_attn(s3(aq), s3(ak), s3(av), bias, slot_last(cache_dk), dv_rows, page_table, li,
                        lam_rows, g_head, lam_init, diff=True)
    b_out = _paged_attn(s3(bq), s3(bk), s3(bv), bias, slot_last(cache_sk), slot_last(cache_sv), page_table, li,
                        lam_rows, g_head, lam_init, diff=False)
    mix_s = [a_out.reshape(bd * tn, HEAD_W), b_out.reshape(bd * tn, HEAD_W)]
    new_s = (ak.reshape(bd, tn, DIFF_HEADS, 2, DIFF_QK_DIM), av.reshape(bd, tn, DIFF_HEADS, DIFF_V_DIM),
             bk.reshape(bd, tn, DSA_HEADS, DSA_HEAD_DIM), bv.reshape(bd, tn, DSA_HEADS, DSA_HEAD_DIM),
             tail.reshape(bd, tn, LANES)[:, :, :IDX_DIM])
    return mix_p, mix_s, w_out, new_p, new_s


def _mlstm_layer(y, bsz, t, w_in, b_i, b_f, g_head, g_mix, c0, n0, m0):
    d = y.shape[1]
    nh = MLSTM_HEADS
    dk = d // (2 * nh)
    dv = d // nh
    segs = ((0, nh * dk, "plain"), (nh * dk, nh * dk, "plain"), (2 * nh * dk, nh * dv, "plain"),
            (2 * nh * dk + nh * dv, nh * dv, "plain"), (2 * nh * dk + 2 * nh * dv, LANES, "plain"))
    q, k, v, og, gts = _norm_proj(y, g_mix, w_in, segs)
    r3 = lambda x: x.reshape(bsz, t, x.shape[-1])
    hh, c, n, m = _mlstm(r3(q), r3(k), r3(v), r3(og), r3(gts), b_i, b_f, g_head, c0, n0, m0)
    return [hh.reshape(bsz * t, nh * dv)], (c, n, m)


def kernel(x_prompt, x_sample, cache_diff_k, cache_diff_v, cache_dsa_k, cache_dsa_v, cache_idx_k, state_mlstm_c, state_mlstm_n, state_mlstm_m, page_table, g_mix, g_ffn, g_final, w_attn_in, w_attn_out, lambda_q1, lambda_k1, lambda_q2, lambda_k2, g_diff_head, w_mlstm_in, b_mlstm_i, b_mlstm_f, g_mlstm_head, w_mlstm_out, w_ffn_up, w_ffn_down):
    b, t, d = x_prompt.shape
    bd, tn, _ = x_sample.shape
    depth = g_mix.shape[0]
    past = page_table.shape[1] * cache_idx_k.shape[2]
    nh = MLSTM_HEADS
    yp = x_prompt.reshape(b * t, d)
    ys = x_sample.reshape(bd * tn, d)
    new_p, new_s, st_p, st_s = [], [], [], []
    for layer in range(depth):
        li = layer // 2
        last = layer == depth - 1
        if layer % 2 == 0:
            lam_init = 0.8 - 0.6 * math.exp(-0.3 * layer)
            lam_rows = jnp.stack([lambda_q1[li], lambda_k1[li], lambda_q2[li], lambda_k2[li]]).astype(F32)
            mix_p, mix_s, w_out, np_, ns_ = _attn_layer(
                yp, ys, w_attn_in[li], w_attn_out[li], lam_rows, g_diff_head[li], g_mix[layer], lam_init,
                (cache_diff_k, cache_diff_v, cache_dsa_k, cache_dsa_v, cache_idx_k), page_table, li,
                (b, t, bd, tn, past))
            new_p.append(np_)
            new_s.append(ns_)
        else:
            w_in = _pad_cols(w_mlstm_in[li].astype(BF16))
            w_out = w_mlstm_out[li].astype(BF16)
            dk = d // (2 * nh)
            dv = d // nh
            zc = jnp.zeros((b, nh, dv, dk), F32)
            zn = jnp.zeros((b, nh, dk), F32)
            zm = jnp.zeros((b, nh), F32)
            mix_p, sp = _mlstm_layer(yp, b, t, w_in, b_mlstm_i[li], b_mlstm_f[li], g_mlstm_head[li],
                                     g_mix[layer], zc, zn, zm)
            mix_s, ss = _mlstm_layer(ys, bd, tn, w_in, b_mlstm_i[li], b_mlstm_f[li], g_mlstm_head[li],
                                     g_mix[layer], state_mlstm_c[li], state_mlstm_n[li], state_mlstm_m[li])
            st_p.append(sp)
            st_s.append(ss)
        w_up = w_ffn_up[layer].astype(BF16)
        w_down = w_ffn_down[layer].astype(BF16)
        gf = g_final if last else None
        yp = _mix_ffn(mix_p, w_out, yp, g_ffn[layer], w_up, w_down, gf)
        ys = _mix_ffn(mix_s, w_out, ys, g_ffn[layer], w_up, w_down, gf)
    stack = lambda items, idx: jnp.stack([it[idx] for it in items])
    return (yp.reshape(b, t, d), ys.reshape(bd, tn, d),
            stack(new_p, 0), stack(new_p, 1), stack(new_p, 2), stack(new_p, 3), stack(new_p, 4),
            stack(st_p, 0), stack(st_p, 1), stack(st_p, 2),
            stack(new_s, 0), stack(new_s, 1), stack(new_s, 2), stack(new_s, 3), stack(new_s, 4),
            stack(st_s, 0), stack(st_s, 1), stack(st_s, 2))
```
